```python
import jax, jax.numpy as jnp
from jax import lax
import numpy as np

D_MODEL = 2048
BATCH = 16
SEQ = 256
DEPTH = 4
DEC_BATCH = 2
DEC_SEQ = 1024
PAST_LEN = 256

GRID_W = 64
N_MIXERS = 2
NA_HEADS = 16
NA_HEAD_DIM = D_MODEL // NA_HEADS
NA_WIN_ROWS = 8
NA_WIN_COLS = 16
GLA_HEADS = 4
GLA_KEY_DIM = D_MODEL // 2
GLA_VAL_DIM = D_MODEL
GLA_DK = GLA_KEY_DIM // GLA_HEADS
GLA_DV = GLA_VAL_DIM // GLA_HEADS
GLA_GATE_RANK = 16
GLA_GATE_NORM = 16.0
GLA_CHUNK = 64
D_FF = 4 * D_MODEL
ROPE_THETA = 10000.0
EPS = 1e-6
ATTN_Q_BLOCK = 128
N_NA_LAYERS = (DEPTH + 1) // 2
N_GLA_LAYERS = DEPTH // 2
NEG_INF = -1e30

kernel_name = 'hybrid_natten_gla_diffusion_step'


def rmsnorm(x, g):
    xf = x.astype(jnp.float32)
    y = xf * lax.rsqrt(jnp.mean(xf * xf, axis=-1, keepdims=True) + EPS)
    return (y * g.astype(jnp.float32)).astype(x.dtype)


def adaln(cond, w, b):
    m = jax.nn.silu(cond) @ w + b
    m = m.reshape(cond.shape[0], 6, 1, D_MODEL)
    return [m[:, j] for j in range(6)]


def modulate(x, g, shift, scale):
    return rmsnorm(x, g) * (1 + scale) + shift


def split_heads(x, h):
    b, l, _ = x.shape
    return x.reshape(b, l, h, -1).transpose(0, 2, 1, 3)


def merge_heads(x):
    b, h, l, d = x.shape
    return x.transpose(0, 2, 1, 3).reshape(b, l, h * d)


def na_qkv(h, w_qkv):
    b, l, _ = h.shape
    p = (h @ w_qkv).reshape(b, l, 3, NA_HEADS, NA_HEAD_DIM).transpose(2, 0, 3, 1, 4)
    return p[0], p[1], p[2]


def context_attention(q, k, v):
    b, h, l, dh = q.shape
    nb = l // ATTN_Q_BLOCK
    qb = q.reshape(b, h, nb, ATTN_Q_BLOCK, dh).transpose(2, 0, 1, 3, 4)

    def one(qblk):
        s = jnp.einsum('bhqd,bhkd->bhqk', qblk, k).astype(jnp.float32) * (dh ** -0.5)
        p = jax.nn.softmax(s, axis=-1).astype(v.dtype)
        return jnp.einsum('bhqk,bhkd->bhqd', p, v)

    o = lax.map(one, qb)
    return o.transpose(1, 2, 0, 3, 4).reshape(b, h, l, dh)


def neighborhood_attention(q, k, v, k_ctx, v_ctx, rpb):
    b, h, t, dh = q.shape
    rows = t // GRID_W
    kr = min(NA_WIN_ROWS, rows)
    kw = NA_WIN_COLS
    scale = dh ** -0.5
    qg = q.reshape(b, h, rows, GRID_W, dh)
    kg = k.reshape(b, h, rows, GRID_W, dh)
    vg = v.reshape(b, h, rows, GRID_W, dh)
    r = jnp.arange(rows)
    c = jnp.arange(GRID_W)
    row_start = jnp.clip(r - kr // 2, 0, rows - kr)
    row_idx = row_start[:, None] + jnp.arange(kr)[None, :]
    col_start = jnp.clip(c - kw // 2, 0, GRID_W - kw)
    col_in = (c[None, :] >= col_start[:, None]) & (c[None, :] < col_start[:, None] + kw)
    k_band = kg[:, :, row_idx]
    v_band = vg[:, :, row_idx]
    s_lat = jnp.einsum('bhrqd,bhrkwd->bhrqkw', qg, k_band).astype(jnp.float32) * scale
    dr = row_idx - r[:, None] + NA_WIN_ROWS - 1
    dc = jnp.clip(c[None, :] - c[:, None] + kw - 1, 0, 2 * kw - 2)
    bias = rpb[:, dr[:, None, :, None], dc[None, :, None, :]]
    s_lat = jnp.where(col_in[None, None, None, :, None, :],
                      s_lat + bias.astype(jnp.float32)[None], NEG_INF)
    s_ctx = jnp.einsum('bhrqd,bhld->bhrql', qg, k_ctx).astype(jnp.float32) * scale
    n_lat = kr * GRID_W
    s = jnp.concatenate([s_lat.reshape(b, h, rows, GRID_W, n_lat), s_ctx], axis=-1)
    p = jax.nn.softmax(s, axis=-1).astype(v.dtype)
    p_lat = p[..., :n_lat].reshape(b, h, rows, GRID_W, kr, GRID_W)
    p_ctx = p[..., n_lat:]
    o = (jnp.einsum('bhrqkw,bhrkwd->bhrqd', p_lat, v_band)
         + jnp.einsum('bhrql,bhld->bhrqd', p_ctx, v_ctx))
    return o.reshape(b, h, t, dh)


def rope_axial(x):
    t_len, d = x.shape[2], x.shape[3]
    t = jnp.arange(t_len)
    half = d // 2
    n = half // 2
    freqs = ROPE_THETA ** (-jnp.arange(n, dtype=jnp.float32) / n)

    def rot(xa, pos):
        ang = pos.astype(jnp.float32)[:, None] * freqs[None, :]
        cos, sin = jnp.cos(ang), jnp.sin(ang)
        x1, x2 = xa[..., :n], xa[..., n:]
        return jnp.concatenate([x1 * cos - x2 * sin, x1 * sin + x2 * cos], axis=-1)

    return jnp.concatenate([rot(x[..., :half], t // GRID_W), rot(x[..., half:], t % GRID_W)], axis=-1)


def gla_chunked(q, k, v, log_a, s0):
    b, h, l, dk = q.shape
    dv = v.shape[-1]
    n = l // GLA_CHUNK
    rs = lambda z: z.reshape(b, h, n, GLA_CHUNK, z.shape[-1])
    q, k, v, log_a = rs(q), rs(k), rs(v), rs(log_a)
    cum = jnp.cumsum(log_a, axis=3)
    cum_last = cum[:, :, :, -1:]
    q_dec = q * jnp.exp(cum)
    k_dec = k * jnp.exp(-cum)
    mask = jnp.tril(jnp.ones((GLA_CHUNK, GLA_CHUNK), dtype=bool))
    attn = jnp.where(mask, jnp.einsum('bhncd,bhnsd->bhncs', q_dec, k_dec), 0.0)
    o_intra = jnp.einsum('bhncs,bhnsv->bhncv', attn, v)
    kv = jnp.einsum('bhncd,bhncv->bhndv', k * jnp.exp(cum_last - cum), v)
    decay = jnp.exp(cum_last[:, :, :, 0])

    def step(s, inp):
        dec, kvn = inp
        return dec[..., None] * s + kvn, s

    s_final, s_prev = lax.scan(step, s0, (jnp.moveaxis(decay, 2, 0), jnp.moveaxis(kv, 2, 0)))
    s_prev = jnp.moveaxis(s_prev, 0, 2)
    o = o_intra + jnp.einsum('bhncd,bhndv->bhncv', q_dec, s_prev)
    return o.reshape(b, h, l, dv), s_final


def gla_mix(h, w_qkvg, w_a1, w_a2, b_a, g_head, w_o, s_init, latent):
    f32 = jnp.float32
    proj = h @ w_qkvg
    q, k, v, g = jnp.split(proj, [GLA_KEY_DIM, 2 * GLA_KEY_DIM, 2 * GLA_KEY_DIM + GLA_VAL_DIM], axis=-1)
    q = split_heads(q, GLA_HEADS).astype(f32)
    k = split_heads(k, GLA_HEADS).astype(f32)
    v = split_heads(v, GLA_HEADS).astype(f32)
    if latent:
        q = rope_axial(q)
        k = rope_axial(k)
    q = q * (GLA_DK ** -0.5)
    flip = lambda z: jnp.flip(z, axis=2)
    outs, finals = [], []
    for d in range(2):
        z = (h @ w_a1[d]) @ w_a2[d] + b_a[d]
        log_a = split_heads(jax.nn.log_sigmoid(z.astype(f32)) / GLA_GATE_NORM, GLA_HEADS)
        s0 = s_init[:, d].astype(f32)
        if d == 0:
            o, sf = gla_chunked(q, k, v, log_a, s0)
        else:
            o, sf = gla_chunked(flip(q), flip(k), flip(v), flip(log_a), s0)
            o = flip(o)
        outs.append(o)
        finals.append(sf)
    o = outs[0] + outs[1]
    o = o * lax.rsqrt(jnp.mean(o * o, axis=-1, keepdims=True) + EPS) * g_head.astype(f32)
    o = o * jax.nn.silu(split_heads(g, GLA_HEADS).astype(f32))
    out = merge_heads(o).astype(h.dtype) @ w_o
    return out, jnp.stack(finals, axis=1).astype(h.dtype)


def sq_relu_mlp(h, w1, w2):
    return jnp.square(jax.nn.relu(h @ w1)) @ w2


def setup_inputs(seed: int = 0) -> dict:
    key = jax.random.key(seed)
    ks = jax.random.split(key, 24)
    nrm = lambda k, shape, s: jax.random.normal(k, shape, jnp.float32) * s
    d = D_MODEL
    return {
        'x_prompt': nrm(ks[0], (BATCH, SEQ, d), 1.0),
        'x_sample': nrm(ks[1], (DEC_BATCH, DEC_SEQ, d), 1.0),
        'cache_na_k': nrm(ks[2], (DEC_BATCH, N_NA_LAYERS, NA_HEADS, PAST_LEN, NA_HEAD_DIM), 1.0),
        'cache_na_v': nrm(ks[3], (DEC_BATCH, N_NA_LAYERS, NA_HEADS, PAST_LEN, NA_HEAD_DIM), 1.0),
        'state_gla': nrm(ks[4], (DEC_BATCH, N_GLA_LAYERS, 2, GLA_HEADS, GLA_DK, GLA_DV), 1.0),
        'c': nrm(ks[5], (DEC_BATCH, d), 1.0),
        'c_ctx': nrm(ks[6], (d,), 1.0),
        'w_ada': nrm(ks[7], (DEPTH, d, 6 * d), 0.5 * d ** -0.5),
        'b_ada': nrm(ks[8], (DEPTH, 6 * d), 0.01),
        'norm_g': 1.0 + nrm(ks[9], (DEPTH, 4, d), 0.01),
        'w_qkv_na': nrm(ks[10], (N_NA_LAYERS, d, 3 * d), d ** -0.5),
        'w_o_na': nrm(ks[11], (N_NA_LAYERS, d, d), d ** -0.5),
        'rpb_na': nrm(ks[12], (N_NA_LAYERS, NA_HEADS, 2 * NA_WIN_ROWS - 1, 2 * NA_WIN_COLS - 1), 0.1),
        'w_qkvg_gla': nrm(ks[13], (N_GLA_LAYERS, d, 2 * GLA_KEY_DIM + 2 * GLA_VAL_DIM), d ** -0.5),
        'w_a1_gla': nrm(ks[14], (N_GLA_LAYERS, 2, d, GLA_GATE_RANK), d ** -0.5),
        'w_a2_gla': nrm(ks[15], (N_GLA_LAYERS, 2, GLA_GATE_RANK, GLA_KEY_DIM), GLA_GATE_RANK ** -0.5),
        'b_a_gla': nrm(ks[16], (N_GLA_LAYERS, 2, GLA_KEY_DIM), 0.1),
        'g_head_gla': 1.0 + nrm(ks[17], (N_GLA_LAYERS, GLA_DV), 0.01),
        'w_o_gla': nrm(ks[18], (N_GLA_LAYERS, GLA_VAL_DIM, d), GLA_VAL_DIM ** -0.5),
        'w1_mlp': nrm(ks[19], (DEPTH, d, D_FF), d ** -0.5),
        'w2_mlp': nrm(ks[20], (DEPTH, D_FF, d), D_FF ** -0.5),
    }


def reference(x_prompt, x_sample, cache_na_k, cache_na_v, state_gla, c, c_ctx,
              w_ada, b_ada, norm_g, w_qkv_na, w_o_na, rpb_na,
              w_qkvg_gla, w_a1_gla, w_a2_gla, b_a_gla, g_head_gla, w_o_gla,
              w1_mlp, w2_mlp):
    xp, xs = x_prompt, x_sample
    new_k, new_v, new_s = [], [], []
    for l in range(DEPTH):
        i = l // N_MIXERS
        sh1p, sc1p, g1p, sh2p, sc2p, g2p = adaln(c_ctx[None], w_ada[l], b_ada[l])
        sh1s, sc1s, g1s, sh2s, sc2s, g2s = adaln(c, w_ada[l], b_ada[l])
        hp = modulate(xp, norm_g[l, 0], sh1p, sc1p)
        hs = modulate(xs, norm_g[l, 0], sh1s, sc1s)
        if l % N_MIXERS == 0:
            qp, kp, vp = na_qkv(hp, w_qkv_na[i])
            yp = merge_heads(context_attention(qp, kp, vp)) @ w_o_na[i]
            new_k.append(kp)
            new_v.append(vp)
            qs, ks_, vs = na_qkv(hs, w_qkv_na[i])
            os_ = neighborhood_attention(qs, ks_, vs, cache_na_k[:, i], cache_na_v[:, i], rpb_na[i])
            ys = merge_heads(os_) @ w_o_na[i]
        else:
            zero_state = jnp.zeros((xp.shape[0], 2, GLA_HEADS, GLA_DK, GLA_DV), jnp.float32)
            yp, sp = gla_mix(hp, w_qkvg_gla[i], w_a1_gla[i], w_a2_gla[i], b_a_gla[i],
                             g_head_gla[i], w_o_gla[i], zero_state, False)
            new_s.append(sp)
            ys, _ = gla_mix(hs, w_qkvg_gla[i], w_a1_gla[i], w_a2_gla[i], b_a_gla[i],
                            g_head_gla[i], w_o_gla[i], state_gla[:, i], True)
        xp = xp + g1p * rmsnorm(yp, norm_g[l, 1])
        xs = xs + g1s * rmsnorm(ys, norm_g[l, 1])
        mp = sq_relu_mlp(modulate(xp, norm_g[l, 2], sh2p, sc2p), w1_mlp[l], w2_mlp[l])
        ms = sq_relu_mlp(modulate(xs, norm_g[l, 2], sh2s, sc2s), w1_mlp[l], w2_mlp[l])
        xp = xp + g2p * rmsnorm(mp, norm_g[l, 3])
        xs = xs + g2s * rmsnorm(ms, norm_g[l, 3])
    new_na_k = jnp.stack(new_k, axis=1)
    new_na_v = jnp.stack(new_v, axis=1)
    new_gla = jnp.stack(new_s, axis=1)
    return (xp, xs, new_na_k, new_na_v, new_gla)
```

```python
import functools

import jax
import jax.numpy as jnp
from jax import lax
from jax.experimental import pallas as pl
from jax.experimental.pallas import tpu as pltpu

F32 = jnp.float32
BF16 = jnp.bfloat16

D_MODEL = 2048
DEPTH = 4
N_MIXERS = 2
GRID_W = 64
NA_HEADS = 16
NA_DH = 128
NA_WIN_ROWS = 8
NA_WIN_COLS = 16
GLA_HEADS = 4
GLA_DK = 256
GLA_DV = 512
GLA_KEY_DIM = GLA_HEADS * GLA_DK
GLA_RANK = 16
GLA_GATE_NORM = 16.0
GLA_CHUNK = 64
D_FF = 4 * D_MODEL
ROPE_THETA = 10000.0
EPS = 1e-6
NEG_INF = -1e30

MOD_ROWS = 8
LANES = 128
TM = 1024
TN = 512
TF = 512
TM_OUT = 512
ROW_CHUNK = 128
VMEM_LIMIT = 58 * 1024 * 1024


def _params(n_axes):
    return pltpu.CompilerParams(
        dimension_semantics=("arbitrary",) * n_axes, vmem_limit_bytes=VMEM_LIMIT)


def _dot(a, b):
    return jnp.dot(a, b, preferred_element_type=F32)


def _dot_nt(a, b):
    return lax.dot_general(a, b, (((1,), (1,)), ((), ())), preferred_element_type=F32)


def _dot_tn(a, b, precision=None):
    return lax.dot_general(a, b, (((0,), (0,)), ((), ())), precision=precision,
                           preferred_element_type=F32)


def _ada_body(c_ref, w_ref, b_ref, o_ref):
    c = c_ref[...]
    s = c * jax.nn.sigmoid(c)
    o_ref[0, 0] = _dot(s.astype(BF16), w_ref[0].astype(BF16)) + b_ref[0, 0]


def _ada_all(cond, w_ada, b_ada):
    return pl.pallas_call(
        _ada_body,
        grid=(DEPTH, 6),
        in_specs=[
            pl.BlockSpec((MOD_ROWS, D_MODEL), lambda l, j: (0, 0)),
            pl.BlockSpec((1, D_MODEL, D_MODEL), lambda l, j: (l, 0, j)),
            pl.BlockSpec((1, 1, 1, D_MODEL), lambda l, j: (l, j, 0, 0)),
        ],
        out_specs=pl.BlockSpec((1, 1, MOD_ROWS, D_MODEL), lambda l, j: (l, j, 0, 0)),
        out_shape=jax.ShapeDtypeStruct((DEPTH, 6, MOD_ROWS, D_MODEL), F32),
        compiler_params=_params(2),
        name="adaln",
    )(cond, w_ada, b_ada.reshape(DEPTH, 6, 1, D_MODEL))


def _mod_row(ref, row):
    return ref[0, 0, pl.ds(row, 1), :]


def _norm_mod_to_scratch(x_ref, g_ref, shift_ref, scale_ref, row, h_scr):
    g = g_ref[0, 0]
    one_plus = 1.0 + _mod_row(scale_ref, row)
    shift = _mod_row(shift_ref, row)

    def step(r, carry):
        rows = pl.ds(pl.multiple_of(r * ROW_CHUNK, ROW_CHUNK), ROW_CHUNK)
        x = x_ref[rows, :]
        y = x * lax.rsqrt(jnp.mean(x * x, axis=-1, keepdims=True) + EPS)
        h_scr[rows, :] = ((y * g) * one_plus + shift).astype(BF16)
        return carry

    lax.fori_loop(0, x_ref.shape[0] // ROW_CHUNK, step, 0)


def _gated_residual(x_ref, y_ref, gn_ref, gate_ref, row, o_ref):
    gn = gn_ref[0, 0]
    gate = _mod_row(gate_ref, row)

    def step(r, carry):
        rows = pl.ds(pl.multiple_of(r * ROW_CHUNK, ROW_CHUNK), ROW_CHUNK)
        y = y_ref[rows, :]
        yn = y * lax.rsqrt(jnp.mean(y * y, axis=-1, keepdims=True) + EPS)
        o_ref[rows, :] = x_ref[rows, :] + gate * (yn * gn)
        return carry

    lax.fori_loop(0, x_ref.shape[0] // ROW_CHUNK, step, 0)


def _mod_spec(layer, j):
    return pl.BlockSpec((1, 1, MOD_ROWS, D_MODEL), lambda *_: (layer, j, 0, 0))


def _gain_spec(layer, j):
    return pl.BlockSpec((1, 1, 1, D_MODEL), lambda *_: (layer, j, 0, 0))


def _proj_body(x_ref, g_ref, shift_ref, scale_ref, w_ref, *rest, row0, row_stride, n_extra, write):
    extra_w = rest[:n_extra]
    outs = rest[n_extra:-1]
    h_scr = rest[-1]
    i = pl.program_id(0)
    n = pl.program_id(1)

    @pl.when(n == 0)
    def _():
        _norm_mod_to_scratch(x_ref, g_ref, shift_ref, scale_ref, row0 + i * row_stride, h_scr)
        if n_extra:
            outs[-1][...] = _dot(h_scr[...], extra_w[0][...].astype(BF16))

    res = _dot(h_scr[...], w_ref[...].astype(BF16))
    write(res, outs, n)


def _write_heads(o_ref, res, seq):
    for b in range(o_ref.shape[0]):
        for j in range(o_ref.shape[1]):
            o_ref[b, j] = res[b * seq:(b + 1) * seq, j * NA_DH:(j + 1) * NA_DH].astype(o_ref.dtype)


def _write_qkv_prompt(res, outs, n, *, seq):
    q_ref, k_ref, v_ref = outs
    nq = D_MODEL // TN

    @pl.when(n < nq)
    def _():
        _write_heads(q_ref, res, seq)

    @pl.when(jnp.logical_and(n >= nq, n < 2 * nq))
    def _():
        _write_heads(k_ref, res, seq)

    @pl.when(n >= 2 * nq)
    def _():
        _write_heads(v_ref, res, seq)


def _write_qkv_sample(res, outs, n, *, seq):
    _write_heads(outs[0], res, seq)


def _write_plain(res, outs, n):
    outs[0][...] = res


def _proj_call(x, norm_g, mods, w, layer, w_layer, *, row0, row_stride, write, out_specs,
               out_shape, extra=None, name):
    m = x.shape[0]
    n_cols = w.shape[-1]
    in_specs = [
        pl.BlockSpec((TM, D_MODEL), lambda i, n: (i, 0)),
        _gain_spec(layer, 0),
        _mod_spec(layer, 0),
        _mod_spec(layer, 1),
        pl.BlockSpec((None, D_MODEL, TN), lambda i, n: (w_layer, 0, n)),
    ]
    args = [x, norm_g, mods, mods, w]
    if extra is not None:
        in_specs.append(pl.BlockSpec(extra.shape, lambda i, n: (0, 0)))
        args.append(extra)
    body = functools.partial(_proj_body, row0=row0, row_stride=row_stride,
                             n_extra=0 if extra is None else 1, write=write)
    return pl.pallas_call(
        body,
        grid=(m // TM, n_cols // TN),
        in_specs=in_specs,
        out_specs=out_specs,
        out_shape=out_shape,
        scratch_shapes=[pltpu.VMEM((TM, D_MODEL), BF16)],
        compiler_params=_params(2),
        name=name,
    )(*args)


def _qkv_prompt(x, norm_g, mods, w_qkv, layer, i_mix, seq):
    nb = x.shape[0] // seq
    tb = TM // seq
    th = TN // NA_DH
    nq = D_MODEL // TN
    shape = (nb, NA_HEADS, seq, NA_DH)
    blk = (tb, th, seq, NA_DH)
    return _proj_call(
        x, norm_g, mods, w_qkv, layer, i_mix, row0=0, row_stride=0,
        write=functools.partial(_write_qkv_prompt, seq=seq),
        out_specs=[
            pl.BlockSpec(blk, lambda i, n: (i, jnp.minimum(n, nq - 1), 0, 0)),
            pl.BlockSpec(blk, lambda i, n: (i, jnp.clip(n - nq, 0, nq - 1), 0, 0)),
            pl.BlockSpec(blk, lambda i, n: (i, jnp.clip(n - 2 * nq, 0, nq - 1), 0, 0)),
        ],
        out_shape=[jax.ShapeDtypeStruct(shape, BF16), jax.ShapeDtypeStruct(shape, F32),
                   jax.ShapeDtypeStruct(shape, F32)],
        name="qkv_prompt")


def _qkv_sample(x, norm_g, mods, w_qkv, layer, i_mix, seq):
    nb = x.shape[0] // seq
    return _proj_call(
        x, norm_g, mods, w_qkv, layer, i_mix, row0=1, row_stride=TM // seq,
        write=functools.partial(_write_qkv_sample, seq=seq),
        out_specs=[pl.BlockSpec((TM // seq, TN // NA_DH, seq, NA_DH), lambda i, n: (i, n, 0, 0))],
        out_shape=[jax.ShapeDtypeStruct((nb, 3 * NA_HEADS, seq, NA_DH), BF16)],
        name="qkv_sample")[0]


def _gla_proj(x, norm_g, mods, w_qkvg, w_a1_pad, layer, i_mix, *, row0, row_stride, name):
    m = x.shape[0]
    n_cols = w_qkvg.shape[-1]
    return _proj_call(
        x, norm_g, mods, w_qkvg, layer, i_mix, row0=row0, row_stride=row_stride,
        write=_write_plain,
        out_specs=[pl.BlockSpec((TM, TN), lambda i, n: (i, n)),
                   pl.BlockSpec((TM, LANES), lambda i, n: (i, 0))],
        out_shape=[jax.ShapeDtypeStruct((m, n_cols), F32), jax.ShapeDtypeStruct((m, LANES), F32)],
        extra=w_a1_pad, name=name)


def _oproj_body(a_ref, w_ref, x_ref, gn_ref, gate_ref, o_ref, w_scr, y_scr, *, row0,
                tiles_per_group):
    i = pl.program_id(0)

    @pl.when(i == 0)
    def _():
        w_scr[...] = w_ref[...].astype(BF16)

    y_scr[...] = _dot(a_ref[...], w_scr[...])
    _gated_residual(x_ref, y_scr, gn_ref, gate_ref, row0 + i // tiles_per_group, o_ref)


def _oproj(a, w, w_layer, x, norm_g, mods, layer, *, row0, n_groups, name):
    m = x.shape[0]
    tiles_per_group = m // n_groups // TM_OUT
    body = functools.partial(_oproj_body, row0=row0, tiles_per_group=tiles_per_group)
    return pl.pallas_call(
        body,
        grid=(m // TM_OUT,),
        in_specs=[
            pl.BlockSpec((TM_OUT, D_MODEL), lambda i: (i, 0)),
            pl.BlockSpec((None, D_MODEL, D_MODEL), lambda i: (w_layer, 0, 0),
                         pipeline_mode=pl.Buffered(1)),
            pl.BlockSpec((TM_OUT, D_MODEL), lambda i: (i, 0)),
            _gain_spec(layer, 1),
            _mod_spec(layer, 2),
        ],
        out_specs=pl.BlockSpec((TM_OUT, D_MODEL), lambda i: (i, 0)),
        out_shape=jax.ShapeDtypeStruct((m, D_MODEL), F32),
        scratch_shapes=[pltpu.VMEM((D_MODEL, D_MODEL), BF16), pltpu.VMEM((TM_OUT, D_MODEL), F32)],
        compiler_params=_params(1),
        name=name,
    )(a, w, x, norm_g, mods)


def _mlp_body(x_ref, g_ref, shift_ref, scale_ref, gn_ref, gate_ref, w1_ref, w2_ref, o_ref, h_scr,
              *, row0, row_stride):
    i = pl.program_id(0)
    f = pl.program_id(1)
    row = row0 + i * row_stride

    @pl.when(f == 0)
    def _():
        _norm_mod_to_scratch(x_ref, g_ref, shift_ref, scale_ref, row, h_scr)

    a = _dot(h_scr[...], w1_ref[...].astype(BF16))
    a = jnp.square(jnp.maximum(a, 0.0)).astype(BF16)
    y = _dot(a, w2_ref[...].astype(BF16))

    @pl.when(f == 0)
    def _():
        o_ref[...] = y

    @pl.when(f > 0)
    def _():
        o_ref[...] += y

    @pl.when(f == pl.num_programs(1) - 1)
    def _():
        _gated_residual(x_ref, o_ref, gn_ref, gate_ref, row, o_ref)


def _mlp(x, norm_g, mods, w1, w2, layer, *, row0, row_stride, name):
    m = x.shape[0]
    body = functools.partial(_mlp_body, row0=row0, row_stride=row_stride)
    return pl.pallas_call(
        body,
        grid=(m // TM, D_FF // TF),
        in_specs=[
            pl.BlockSpec((TM, D_MODEL), lambda i, f: (i, 0), pipeline_mode=pl.Buffered(1)),
            _gain_spec(layer, 2),
            _mod_spec(layer, 3),
            _mod_spec(layer, 4),
            _gain_spec(layer, 3),
            _mod_spec(layer, 5),
            pl.BlockSpec((None, D_MODEL, TF), lambda i, f: (layer, 0, f)),
            pl.BlockSpec((None, TF, D_MODEL), lambda i, f: (layer, f, 0)),
        ],
        out_specs=pl.BlockSpec((TM, D_MODEL), lambda i, f: (i, 0)),
        out_shape=jax.ShapeDtypeStruct((m, D_MODEL), F32),
        scratch_shapes=[pltpu.VMEM((TM, D_MODEL), BF16)],
        compiler_params=_params(2),
        name=name,
    )(x, norm_g, mods, mods, norm_g, mods, w1, w2)


def _softmax_rows(parts):
    m = parts[0].max(axis=-1, keepdims=True)
    for s in parts[1:]:
        m = jnp.maximum(m, s.max(axis=-1, keepdims=True))
    es = [jnp.exp(s - m) for s in parts]
    den = es[0].sum(axis=-1, keepdims=True)
    for e in es[1:]:
        den = den + e.sum(axis=-1, keepdims=True)
    return [e / den for e in es]


def _ctx_attn_body(q_ref, k_ref, v_ref, o_ref):
    scale = NA_DH ** -0.5
    for h in range(NA_HEADS):
        q = q_ref[0, h]
        k = k_ref[0, h].astype(BF16)
        v = v_ref[0, h].astype(BF16)
        (p,) = _softmax_rows([_dot_nt(q, k) * scale])
        o_ref[:, h * NA_DH:(h + 1) * NA_DH] = _dot(p.astype(BF16), v).astype(BF16)


def _ctx_attn(q, k, v):
    nb, nh, seq, dh = q.shape
    spec = pl.BlockSpec((1, nh, seq, dh), lambda b: (b, 0, 0, 0))
    return pl.pallas_call(
        _ctx_attn_body,
        grid=(nb,),
        in_specs=[spec, spec, spec],
        out_specs=pl.BlockSpec((seq, nh * dh), lambda b: (b, 0)),
        out_shape=jax.ShapeDtypeStruct((nb * seq, nh * dh), BF16),
        compiler_params=_params(1),
        name="ctx_attn",
    )(q, k, v)


def _na_bias_table(rpb):
    kw = NA_WIN_COLS
    c = jnp.arange(GRID_W)
    col_start = jnp.clip(c - kw // 2, 0, GRID_W - kw)
    col_in = (c[None, :] >= col_start[:, None]) & (c[None, :] < col_start[:, None] + kw)
    dc = jnp.clip(c[None, :] - c[:, None] + kw - 1, 0, 2 * kw - 2)
    t = jnp.where(col_in[None, None], rpb[:, :, dc], NEG_INF)
    return t.transpose(0, 2, 1, 3).reshape(rpb.shape[0], GRID_W, -1)


def _na_attn_body(q_ref, k_ref, v_ref, kc_ref, vc_ref, bias_ref, o_ref, *, rows):
    scale = NA_DH ** -0.5
    kr = min(NA_WIN_ROWS, rows)
    kc = kc_ref[0, 0, 0].astype(BF16)
    vc = vc_ref[0, 0, 0].astype(BF16)
    for r in range(rows):
        rs = min(max(r - kr // 2, 0), rows - kr)
        dr0 = rs - r + NA_WIN_ROWS - 1
        q = q_ref[0, 0, r * GRID_W:(r + 1) * GRID_W, :]
        kb = k_ref[0, 0, rs * GRID_W:(rs + kr) * GRID_W, :]
        vb = v_ref[0, 0, rs * GRID_W:(rs + kr) * GRID_W, :]
        bias = bias_ref[0, :, dr0 * GRID_W:(dr0 + kr) * GRID_W]
        s_lat = _dot_nt(q, kb) * scale + bias
        s_ctx = _dot_nt(q, kc) * scale
        p_lat, p_ctx = _softmax_rows([s_lat, s_ctx])
        o = _dot(p_lat.astype(BF16), vb) + _dot(p_ctx.astype(BF16), vc)
        o_ref[r * GRID_W:(r + 1) * GRID_W, :] = o.astype(BF16)


def _na_attn(qkv, cache_k, cache_v, bias_tab, i_mix):
    nb, _, t, dh = qkv.shape
    past = cache_k.shape[3]

    def qkv_spec(part):
        return pl.BlockSpec((1, 1, t, dh), lambda b, h: (b, part * NA_HEADS + h, 0, 0))

    cache_spec = pl.BlockSpec((1, 1, 1, past, dh), lambda b, h: (b, i_mix, h, 0, 0))
    return pl.pallas_call(
        functools.partial(_na_attn_body, rows=t // GRID_W),
        grid=(nb, NA_HEADS),
        in_specs=[qkv_spec(0), qkv_spec(1), qkv_spec(2), cache_spec, cache_spec,
                  pl.BlockSpec((1,) + bias_tab.shape[1:], lambda b, h: (h, 0, 0))],
        out_specs=pl.BlockSpec((t, dh), lambda b, h: (b, h)),
        out_shape=jax.ShapeDtypeStruct((nb * t, NA_HEADS * dh), BF16),
        compiler_params=_params(2),
        name="na_attn",
    )(qkv, qkv, qkv, cache_k, cache_v, bias_tab)


def _rope_tables(t_len):
    t = jnp.arange(t_len)
    n = GLA_DK // 4
    freqs = ROPE_THETA ** (-jnp.arange(n, dtype=F32) / n)

    def tab(pos):
        ang = pos.astype(F32)[:, None] * freqs[None, :]
        cos, sin = jnp.cos(ang), jnp.sin(ang)
        return jnp.concatenate([cos, cos], axis=-1), jnp.concatenate([-sin, sin], axis=-1)

    c_row, s_row = tab(t // GRID_W)
    c_col, s_col = tab(t % GRID_W)
    return jnp.concatenate([c_row, c_col], axis=-1), jnp.concatenate([s_row, s_col], axis=-1)


def _rope(x, cos, sin):
    half = GLA_DK // 2
    swapped = jnp.concatenate(
        [pltpu.roll(x[:, :half], half // 2, axis=1), pltpu.roll(x[:, half:], half // 2, axis=1)],
        axis=1)
    return x * cos + swapped * sin


def _gla_body(*refs, seq, latent):
    if latent:
        (q_ref, k_ref, v_ref, g_ref, lr_ref, wa2_ref, ba_ref, gh_ref, cos_ref, sin_ref, s0_ref,
         o_ref, q_scr, k_scr, la_scr, of_scr, s_scr) = refs
    else:
        (q_ref, k_ref, v_ref, g_ref, lr_ref, wa2_ref, ba_ref, gh_ref,
         o_ref, sf_ref, q_scr, k_scr, la_scr, of_scr, s_scr) = refs
    n_chunks = seq // GLA_CHUNK
    c = GLA_CHUNK

    q = q_ref[...]
    k = k_ref[...]
    if latent:
        q = _rope(q, cos_ref[...], sin_ref[...])
        k = _rope(k, cos_ref[...], sin_ref[...])
    q_scr[...] = q * (GLA_DK ** -0.5)
    k_scr[...] = k
    lr = lr_ref[...].astype(BF16)
    for d in range(2):
        z = _dot(lr, wa2_ref[d].astype(BF16)) + ba_ref[d]
        log_sig = jnp.minimum(z, 0.0) - jnp.log1p(jnp.exp(-jnp.abs(z)))
        la_scr[d] = log_sig * (1.0 / GLA_GATE_NORM)

    r_i = lax.broadcasted_iota(jnp.int32, (c, c), 0)
    c_i = lax.broadcasted_iota(jnp.int32, (c, c), 1)
    ones_cols = jnp.ones((c, LANES), F32)
    gh = gh_ref[...]

    def chunk(d, rows):
        keep = (c_i <= r_i) if d == 0 else (c_i >= r_i)
        la = la_scr[d, rows, :]
        cum = jnp.dot(keep.astype(F32), la, precision=lax.Precision.HIGHEST,
                      preferred_element_type=F32)
        tot = cum[c - 1:c, :] if d == 0 else cum[0:1, :]
        tot_col = _dot_tn(la, ones_cols, precision=lax.Precision.HIGHEST)[:, 0:1]
        kk = k_scr[rows, :]
        qd = (q_scr[rows, :] * jnp.exp(cum)).astype(BF16)
        kd = (kk * jnp.exp(-cum)).astype(BF16)
        k2 = (kk * jnp.exp(tot - cum)).astype(BF16)
        v = v_ref[rows, :].astype(BF16)
        attn = jnp.where(keep, _dot_nt(qd, kd), 0.0).astype(BF16)
        s_prev = s_scr[...]
        o = _dot(attn, v) + _dot(qd, s_prev.astype(BF16))
        s_scr[...] = jnp.exp(tot_col) * s_prev + _dot_tn(k2, v)
        return o

    def rows_of(ci):
        return pl.ds(pl.multiple_of(ci * c, c), c)

    s_scr[...] = s0_ref[0, 0, 0, 0] if latent else jnp.zeros_like(s_scr)

    def fwd(ci, carry):
        rows = rows_of(ci)
        of_scr[rows, :] = chunk(0, rows)
        return carry

    lax.fori_loop(0, n_chunks, fwd, 0)
    if not latent:
        sf_ref[0, 0, 0] = s_scr[...]

    s_scr[...] = s0_ref[0, 0, 1, 0] if latent else jnp.zeros_like(s_scr)

    def bwd(j, carry):
        rows = rows_of(n_chunks - 1 - j)
        o = of_scr[rows, :] + chunk(1, rows)
        o = o * lax.rsqrt(jnp.mean(o * o, axis=-1, keepdims=True) + EPS) * gh
        g = g_ref[rows, :]
        o_ref[rows, :] = (o * (g * jax.nn.sigmoid(g))).astype(BF16)
        return carry

    lax.fori_loop(0, n_chunks, bwd, 0)
    if not latent:
        sf_ref[0, 1, 0] = s_scr[...]


def _gla(proj, lr, w_a2_pad, b_a, g_head, i_mix, *, seq, rope=None, s0=None):
    m = proj.shape[0]
    nb = m // seq
    latent = s0 is not None
    kq = GLA_KEY_DIM // GLA_DK
    kv = 2 * GLA_KEY_DIM // GLA_DV
    in_specs = [
        pl.BlockSpec((seq, GLA_DK), lambda b, h: (b, h)),
        pl.BlockSpec((seq, GLA_DK), lambda b, h: (b, kq + h)),
        pl.BlockSpec((seq, GLA_DV), lambda b, h: (b, kv + h)),
        pl.BlockSpec((seq, GLA_DV), lambda b, h: (b, kv + GLA_HEADS + h)),
        pl.BlockSpec((seq, LANES), lambda b, h: (b, 0)),
        pl.BlockSpec((None, 2, LANES, GLA_DK), lambda b, h: (i_mix, 0, 0, h)),
        pl.BlockSpec((None, 2, 1, GLA_DK), lambda b, h: (i_mix, 0, 0, h)),
        pl.BlockSpec((None, 1, GLA_DV), lambda b, h: (i_mix, 0, 0)),
    ]
    args = [proj, proj, proj, proj, lr, w_a2_pad, b_a, g_head]
    a_spec = pl.BlockSpec((seq, GLA_DV), lambda b, h: (b, h))
    a_shape = jax.ShapeDtypeStruct((m, GLA_HEADS * GLA_DV), BF16)
    if latent:
        in_specs += [
            pl.BlockSpec((seq, GLA_DK), lambda b, h: (0, 0)),
            pl.BlockSpec((seq, GLA_DK), lambda b, h: (0, 0)),
            pl.BlockSpec((1, 1, 2, 1, GLA_DK, GLA_DV), lambda b, h: (b, i_mix, 0, h, 0, 0)),
        ]
        args += [rope[0], rope[1], s0]
        out_specs, out_shape = a_spec, a_shape
    else:
        out_specs = [a_spec, pl.BlockSpec((1, 2, 1, GLA_DK, GLA_DV), lambda b, h: (b, 0, h, 0, 0))]
        out_shape = [a_shape, jax.ShapeDtypeStruct((nb, 2, GLA_HEADS, GLA_DK, GLA_DV), F32)]
    return pl.pallas_call(
        functools.partial(_gla_body, seq=seq, latent=latent),
        grid=(nb, GLA_HEADS),
        in_specs=in_specs,
        out_specs=out_specs,
        out_shape=out_shape,
        scratch_shapes=[
            pltpu.VMEM((seq, GLA_DK), F32), pltpu.VMEM((seq, GLA_DK), F32),
            pltpu.VMEM((2, seq, GLA_DK), F32), pltpu.VMEM((seq, GLA_DV), F32),
            pltpu.VMEM((GLA_DK, GLA_DV), F32)],
        compiler_params=_params(2),
        name="gla_sample" if latent else "gla_prompt",
    )(*args)


def kernel(x_prompt, x_sample, cache_na_k, cache_na_v, state_gla, c, c_ctx, w_ada, b_ada, norm_g,
           w_qkv_na, w_o_na, rpb_na, w_qkvg_gla, w_a1_gla, w_a2_gla, b_a_gla, g_head_gla, w_o_gla,
           w1_mlp, w2_mlp):
    n_p, seq_p, _ = x_prompt.shape
    n_s, seq_s, _ = x_sample.shape
    assert seq_s == TM and TM % seq_p == 0 and 1 + n_s <= MOD_ROWS
    xp = x_prompt.reshape(n_p * seq_p, D_MODEL)
    xs = x_sample.reshape(n_s * seq_s, D_MODEL)

    cond = jnp.zeros((MOD_ROWS, D_MODEL), F32).at[0].set(c_ctx).at[1:1 + n_s].set(c)
    mods = _ada_all(cond, w_ada, b_ada)
    norm_g4 = norm_g.reshape(DEPTH, 4, 1, D_MODEL)

    n_gla = w_a1_gla.shape[0]
    w_a1_pad = jnp.zeros((n_gla, D_MODEL, LANES), F32).at[:, :, :2 * GLA_RANK].set(
        w_a1_gla.transpose(0, 2, 1, 3).reshape(n_gla, D_MODEL, 2 * GLA_RANK))
    w_a2_pad = jnp.zeros((n_gla, 2, LANES, GLA_KEY_DIM), F32)
    for d in range(2):
        w_a2_pad = w_a2_pad.at[:, d, d * GLA_RANK:(d + 1) * GLA_RANK].set(w_a2_gla[:, d])
    b_a4 = b_a_gla.reshape(n_gla, 2, 1, GLA_KEY_DIM)
    g_head3 = g_head_gla.reshape(n_gla, 1, GLA_DV)
    rope = _rope_tables(seq_s)

    new_k, new_v, new_s = [], [], []
    for l in range(DEPTH):
        i = l // N_MIXERS
        if l % N_MIXERS == 0:
            q_p, k_p, v_p = _qkv_prompt(xp, norm_g4, mods, w_qkv_na, l, i, seq_p)
            new_k.append(k_p)
            new_v.append(v_p)
            a_p = _ctx_attn(q_p, k_p, v_p)
            qkv_s = _qkv_sample(xs, norm_g4, mods, w_qkv_na, l, i, seq_s)
            a_s = _na_attn(qkv_s, cache_na_k, cache_na_v, _na_bias_table(rpb_na[i]), i)
            w_o = w_o_na
        else:
            proj_p, lr_p = _gla_proj(xp, norm_g4, mods, w_qkvg_gla, w_a1_pad[i], l, i,
                                     row0=0, row_stride=0, name="gla_proj_prompt")
            a_p, s_p = _gla(proj_p, lr_p, w_a2_pad, b_a4, g_head3, i, seq=seq_p)
            new_s.append(s_p)
            proj_s, lr_s = _gla_proj(xs, norm_g4, mods, w_qkvg_gla, w_a1_pad[i], l, i,
                                     row0=1, row_stride=TM // seq_s, name="gla_proj_sample")
            a_s = _gla(proj_s, lr_s, w_a2_pad, b_a4, g_head3, i, seq=seq_s, rope=rope,
                       s0=state_gla)
            w_o = w_o_gla
        xp = _oproj(a_p, w_o, i, xp, norm_g4, mods, l, row0=0, n_groups=1, name="oproj_prompt")
        xs = _oproj(a_s, w_o, i, xs, norm_g4, mods, l, row0=1, n_groups=n_s, name="oproj_sample")
        xp = _mlp(xp, norm_g4, mods, w1_mlp, w2_mlp, l, row0=0, row_stride=0, name="mlp_prompt")
        xs = _mlp(xs, norm_g4, mods, w1_mlp, w2_mlp, l, row0=1, row_stride=TM // seq_s,
                  name="mlp_sample")

    return (xp.reshape(n_p, seq_p, D_MODEL), xs.reshape(n_s, seq_s, D_MODEL),
            jnp.stack(new_k, axis=1), jnp.stack(new_v, axis=1), jnp.stack(new_s, axis=1))
```

```python
import functools

import jax
import jax.numpy as jnp
from jax import lax
from jax.experimental import pallas as pl
from jax.experimental.pallas import tpu as pltpu

F32 = jnp.float32
BF16 = jnp.bfloat16

D_MODEL = 2048
DEPTH = 4
N_MIXERS = 2
GRID_W = 64
NA_HEADS = 16
NA_DH = 128
NA_WIN_ROWS = 8
NA_WIN_COLS = 16
GLA_HEADS = 4
GLA_DK = 256
GLA_DV = 512
GLA_KEY_DIM = GLA_HEADS * GLA_DK
GLA_RANK = 16
GLA_GATE_NORM = 16.0
GLA_CHUNK = 64
D_FF = 4 * D_MODEL
ROPE_THETA = 10000.0
EPS = 1e-6
NEG_INF = -1e30

MOD_ROWS = 8
LANES = 128
TM = 1024
TN = 512
TF = 512
TM_OUT = 512
ROW_CHUNK = 128
NA_GROUP = 4
GLA_BLOCK = 256
VMEM_LIMIT = 58 * 1024 * 1024


def _params(n_axes):
    return pltpu.CompilerParams(
        dimension_semantics=("arbitrary",) * n_axes, vmem_limit_bytes=VMEM_LIMIT)


def _dot(a, b):
    return jnp.dot(a, b, preferred_element_type=F32)


def _dot_nt(a, b):
    return lax.dot_general(a, b, (((1,), (1,)), ((), ())), preferred_element_type=F32)


def _dot_tn(a, b):
    return lax.dot_general(a, b, (((0,), (0,)), ((), ())), preferred_element_type=F32)


def _ada_body(c_ref, w_ref, b_ref, o_ref):
    c = c_ref[...]
    s = c * jax.nn.sigmoid(c)
    o_ref[0, 0] = _dot(s.astype(BF16), w_ref[0].astype(BF16)) + b_ref[0, 0]


def _ada_all(cond, w_ada, b_ada):
    return pl.pallas_call(
        _ada_body,
        grid=(DEPTH, 6),
        in_specs=[
            pl.BlockSpec((MOD_ROWS, D_MODEL), lambda l, j: (0, 0)),
            pl.BlockSpec((1, D_MODEL, D_MODEL), lambda l, j: (l, 0, j)),
            pl.BlockSpec((1, 1, 1, D_MODEL), lambda l, j: (l, j, 0, 0)),
        ],
        out_specs=pl.BlockSpec((1, 1, MOD_ROWS, D_MODEL), lambda l, j: (l, j, 0, 0)),
        out_shape=jax.ShapeDtypeStruct((DEPTH, 6, MOD_ROWS, D_MODEL), F32),
        compiler_params=_params(2),
        name="adaln",
    )(cond, w_ada, b_ada.reshape(DEPTH, 6, 1, D_MODEL))


def _mod_row(ref, row):
    return ref[0, 0, pl.ds(row, 1), :]


def _norm_mod_to_scratch(x_ref, g_ref, shift_ref, scale_ref, row, h_scr):
    g = g_ref[0, 0]
    one_plus = 1.0 + _mod_row(scale_ref, row)
    shift = _mod_row(shift_ref, row)

    def step(r, carry):
        rows = pl.ds(pl.multiple_of(r * ROW_CHUNK, ROW_CHUNK), ROW_CHUNK)
        x = x_ref[rows, :]
        y = x * lax.rsqrt(jnp.mean(x * x, axis=-1, keepdims=True) + EPS)
        h_scr[rows, :] = ((y * g) * one_plus + shift).astype(BF16)
        return carry

    lax.fori_loop(0, x_ref.shape[0] // ROW_CHUNK, step, 0)


def _gated_residual(x_ref, y_ref, gn_ref, gate_ref, row, o_ref):
    gn = gn_ref[0, 0]
    gate = _mod_row(gate_ref, row)

    def step(r, carry):
        rows = pl.ds(pl.multiple_of(r * ROW_CHUNK, ROW_CHUNK), ROW_CHUNK)
        y = y_ref[rows, :]
        yn = y * lax.rsqrt(jnp.mean(y * y, axis=-1, keepdims=True) + EPS)
        o_ref[rows, :] = x_ref[rows, :] + gate * (yn * gn)
        return carry

    lax.fori_loop(0, x_ref.shape[0] // ROW_CHUNK, step, 0)


def _mod_spec(layer, j):
    return pl.BlockSpec((1, 1, MOD_ROWS, D_MODEL), lambda *_: (layer, j, 0, 0))


def _gain_spec(layer, j):
    return pl.BlockSpec((1, 1, 1, D_MODEL), lambda *_: (layer, j, 0, 0))


def _proj_body(x_ref, g_ref, shift_ref, scale_ref, w_ref, *rest, row0, row_stride, n_extra,
               lowrank, write):
    extra = rest[:n_extra]
    outs = rest[n_extra:-1]
    h_scr = rest[-1]
    i = pl.program_id(0)
    n = pl.program_id(1)

    @pl.when(n == 0)
    def _():
        _norm_mod_to_scratch(x_ref, g_ref, shift_ref, scale_ref, row0 + i * row_stride, h_scr)
        if lowrank:
            outs[-1][...] = _dot(h_scr[...], extra[0][...].astype(BF16))

    res = _dot(h_scr[...], w_ref[...].astype(BF16))
    write(res, outs, n)


def _write_heads(o_ref, res, seq):
    for b in range(o_ref.shape[0]):
        for j in range(o_ref.shape[1]):
            o_ref[b, j] = res[b * seq:(b + 1) * seq, j * NA_DH:(j + 1) * NA_DH].astype(o_ref.dtype)


def _write_qkv_prompt(res, outs, n, *, seq):
    q_ref, k_ref, v_ref = outs
    nq = D_MODEL // TN

    @pl.when(n < nq)
    def _():
        _write_heads(q_ref, res, seq)

    @pl.when(jnp.logical_and(n >= nq, n < 2 * nq))
    def _():
        _write_heads(k_ref, res, seq)

    @pl.when(n >= 2 * nq)
    def _():
        _write_heads(v_ref, res, seq)


def _write_qkv_sample(res, outs, n, *, seq):
    _write_heads(outs[0], res, seq)


def _write_plain(res, outs, n):
    outs[0][...] = res


def _proj_call(x, norm_g, mods, w, layer, w_layer, *, row0, row_stride, write, out_specs,
               out_shape, lowrank_w=None, carried=(), aliases=None, name):
    m = x.shape[0]
    n_cols = w.shape[-1]
    in_specs = [
        pl.BlockSpec((TM, D_MODEL), lambda i, n: (i, 0)),
        _gain_spec(layer, 0),
        _mod_spec(layer, 0),
        _mod_spec(layer, 1),
        pl.BlockSpec((None, D_MODEL, TN), lambda i, n: (w_layer, 0, n)),
    ]
    args = [x, norm_g, mods, mods, w]
    if lowrank_w is not None:
        in_specs.append(pl.BlockSpec(lowrank_w.shape, lambda i, n: (0, 0)))
        args.append(lowrank_w)
    for arr in carried:
        in_specs.append(pl.BlockSpec(memory_space=pl.ANY))
        args.append(arr)
    body = functools.partial(_proj_body, row0=row0, row_stride=row_stride,
                             n_extra=len(args) - 5, lowrank=lowrank_w is not None, write=write)
    return pl.pallas_call(
        body,
        grid=(m // TM, n_cols // TN),
        in_specs=in_specs,
        out_specs=out_specs,
        out_shape=out_shape,
        scratch_shapes=[pltpu.VMEM((TM, D_MODEL), BF16)],
        input_output_aliases=aliases or {},
        compiler_params=_params(2),
        name=name,
    )(*args)


def _qkv_prompt(x, norm_g, mods, w_qkv, layer, i_mix, seq, n_mix, prev_kv):
    nb = x.shape[0] // seq
    tb = TM // seq
    th = TN // NA_DH
    nq = D_MODEL // TN
    q_shape = (nb, NA_HEADS, seq, NA_DH)
    kv_shape = (nb, n_mix, NA_HEADS, seq, NA_DH)
    q_blk = (tb, th, seq, NA_DH)
    kv_blk = (tb, None, th, seq, NA_DH)
    return _proj_call(
        x, norm_g, mods, w_qkv, layer, i_mix, row0=0, row_stride=0,
        write=functools.partial(_write_qkv_prompt, seq=seq),
        out_specs=[
            pl.BlockSpec(q_blk, lambda i, n: (i, jnp.minimum(n, nq - 1), 0, 0)),
            pl.BlockSpec(kv_blk, lambda i, n: (i, i_mix, jnp.clip(n - nq, 0, nq - 1), 0, 0)),
            pl.BlockSpec(kv_blk, lambda i, n: (i, i_mix, jnp.clip(n - 2 * nq, 0, nq - 1), 0, 0)),
        ],
        out_shape=[jax.ShapeDtypeStruct(q_shape, BF16), jax.ShapeDtypeStruct(kv_shape, F32),
                   jax.ShapeDtypeStruct(kv_shape, F32)],
        carried=prev_kv, aliases={5: 1, 6: 2} if prev_kv else None,
        name="qkv_prompt")


def _qkv_sample(x, norm_g, mods, w_qkv, layer, i_mix, seq):
    nb = x.shape[0] // seq
    return _proj_call(
        x, norm_g, mods, w_qkv, layer, i_mix, row0=1, row_stride=TM // seq,
        write=functools.partial(_write_qkv_sample, seq=seq),
        out_specs=[pl.BlockSpec((TM // seq, TN // NA_DH, seq, NA_DH), lambda i, n: (i, n, 0, 0))],
        out_shape=[jax.ShapeDtypeStruct((nb, 3 * NA_HEADS, seq, NA_DH), BF16)],
        name="qkv_sample")[0]


def _gla_proj(x, norm_g, mods, w_qkvg, w_a1_pad, layer, i_mix, *, row0, row_stride, name):
    m = x.shape[0]
    n_cols = w_qkvg.shape[-1]
    return _proj_call(
        x, norm_g, mods, w_qkvg, layer, i_mix, row0=row0, row_stride=row_stride,
        write=_write_plain,
        out_specs=[pl.BlockSpec((TM, TN), lambda i, n: (i, n)),
                   pl.BlockSpec((TM, LANES), lambda i, n: (i, 0))],
        out_shape=[jax.ShapeDtypeStruct((m, n_cols), F32), jax.ShapeDtypeStruct((m, LANES), F32)],
        lowrank_w=w_a1_pad, name=name)


def _oproj_body(a_ref, w_ref, x_ref, gn_ref, gate_ref, o_ref, w_scr, y_scr, *, row0,
                tiles_per_group):
    i = pl.program_id(0)

    @pl.when(i == 0)
    def _():
        w_scr[...] = w_ref[...].astype(BF16)

    y_scr[...] = _dot(a_ref[...], w_scr[...])
    _gated_residual(x_ref, y_scr, gn_ref, gate_ref, row0 + i // tiles_per_group, o_ref)


def _oproj(a, w, w_layer, x, norm_g, mods, layer, *, row0, n_groups, name):
    m = x.shape[0]
    tiles_per_group = m // n_groups // TM_OUT
    body = functools.partial(_oproj_body, row0=row0, tiles_per_group=tiles_per_group)
    return pl.pallas_call(
        body,
        grid=(m // TM_OUT,),
        in_specs=[
            pl.BlockSpec((TM_OUT, D_MODEL), lambda i: (i, 0)),
            pl.BlockSpec((None, D_MODEL, D_MODEL), lambda i: (w_layer, 0, 0),
                         pipeline_mode=pl.Buffered(1)),
            pl.BlockSpec((TM_OUT, D_MODEL), lambda i: (i, 0)),
            _gain_spec(layer, 1),
            _mod_spec(layer, 2),
        ],
        out_specs=pl.BlockSpec((TM_OUT, D_MODEL), lambda i: (i, 0)),
        out_shape=jax.ShapeDtypeStruct((m, D_MODEL), F32),
        scratch_shapes=[pltpu.VMEM((D_MODEL, D_MODEL), BF16), pltpu.VMEM((TM_OUT, D_MODEL), F32)],
        compiler_params=_params(1),
        name=name,
    )(a, w, x, norm_g, mods)


def _mlp_body(x_ref, g_ref, shift_ref, scale_ref, gn_ref, gate_ref, w1_ref, w2_ref, o_ref, h_scr,
              *, row0, row_stride):
    i = pl.program_id(0)
    f = pl.program_id(1)
    row = row0 + i * row_stride

    @pl.when(f == 0)
    def _():
        _norm_mod_to_scratch(x_ref, g_ref, shift_ref, scale_ref, row, h_scr)
        o_ref[...] = jnp.zeros_like(o_ref)

    a = _dot(h_scr[...], w1_ref[...].astype(BF16))
    a = jnp.square(jnp.maximum(a, 0.0)).astype(BF16)
    o_ref[...] += _dot(a, w2_ref[...].astype(BF16))

    @pl.when(f == pl.num_programs(1) - 1)
    def _():
        _gated_residual(x_ref, o_ref, gn_ref, gate_ref, row, o_ref)


def _mlp(x, norm_g, mods, w1, w2, layer, *, row0, row_stride, name):
    m = x.shape[0]
    body = functools.partial(_mlp_body, row0=row0, row_stride=row_stride)
    return pl.pallas_call(
        body,
        grid=(m // TM, D_FF // TF),
        in_specs=[
            pl.BlockSpec((TM, D_MODEL), lambda i, f: (i, 0), pipeline_mode=pl.Buffered(1)),
            _gain_spec(layer, 2),
            _mod_spec(layer, 3),
            _mod_spec(layer, 4),
            _gain_spec(layer, 3),
            _mod_spec(layer, 5),
            pl.BlockSpec((None, D_MODEL, TF), lambda i, f: (layer, 0, f)),
            pl.BlockSpec((None, TF, D_MODEL), lambda i, f: (layer, f, 0)),
        ],
        out_specs=pl.BlockSpec((TM, D_MODEL), lambda i, f: (i, 0)),
        out_shape=jax.ShapeDtypeStruct((m, D_MODEL), F32),
        scratch_shapes=[pltpu.VMEM((TM, D_MODEL), BF16)],
        compiler_params=_params(2),
        name=name,
    )(x, norm_g, mods, mods, norm_g, mods, w1, w2)


def _softmax_rows(parts):
    m = parts[0].max(axis=-1, keepdims=True)
    for s in parts[1:]:
        m = jnp.maximum(m, s.max(axis=-1, keepdims=True))
    es = [jnp.exp(s - m) for s in parts]
    den = es[0].sum(axis=-1, keepdims=True)
    for e in es[1:]:
        den = den + e.sum(axis=-1, keepdims=True)
    return [e / den for e in es]


def _ctx_attn_body(q_ref, k_ref, v_ref, o_ref):
    scale = NA_DH ** -0.5
    for h in range(NA_HEADS):
        q = q_ref[0, h]
        k = k_ref[0, h].astype(BF16)
        v = v_ref[0, h].astype(BF16)
        (p,) = _softmax_rows([_dot_nt(q, k) * scale])
        o_ref[:, h * NA_DH:(h + 1) * NA_DH] = _dot(p.astype(BF16), v).astype(BF16)


def _ctx_attn(q, k, v, i_mix):
    nb, nh, seq, dh = q.shape
    kv_spec = pl.BlockSpec((1, None, nh, seq, dh), lambda b: (b, i_mix, 0, 0, 0))
    return pl.pallas_call(
        _ctx_attn_body,
        grid=(nb,),
        in_specs=[pl.BlockSpec((1, nh, seq, dh), lambda b: (b, 0, 0, 0)), kv_spec, kv_spec],
        out_specs=pl.BlockSpec((seq, nh * dh), lambda b: (b, 0)),
        out_shape=jax.ShapeDtypeStruct((nb * seq, nh * dh), BF16),
        compiler_params=_params(1),
        name="ctx_attn",
    )(q, k, v)


N_DR = 2 * NA_WIN_ROWS


def _na_bias_tables(rpb_ref, even_scr, odd_scr):
    assert LANES == 2 * GRID_W
    qc = lax.broadcasted_iota(jnp.int32, (GRID_W, LANES), 0)
    lane = lax.broadcasted_iota(jnp.int32, (GRID_W, LANES), 1)
    kc = lane & (GRID_W - 1)
    col_start = jnp.clip(qc - NA_WIN_COLS // 2, 0, GRID_W - NA_WIN_COLS)
    valid = (kc >= col_start) & (kc < col_start + NA_WIN_COLS)
    first = lane < GRID_W

    def toeplitz(dr, lane_off):
        g = jnp.broadcast_to(rpb_ref[0, dr:dr + 1, :], (GRID_W, LANES))
        shift = (LANES - (NA_WIN_COLS - 1) + lane_off) % LANES
        return pltpu.roll(g, shift, 1, stride=1, stride_axis=0)

    def slab(dr):
        t = jnp.where(first, toeplitz(dr, 0), toeplitz(dr + 1, GRID_W))
        return jnp.where(valid, t, NEG_INF)

    for s in range(N_DR // 2):
        even_scr[:, s * LANES:(s + 1) * LANES] = slab(2 * s)
    for s in range(N_DR // 2 - 1):
        odd_scr[:, s * LANES:(s + 1) * LANES] = slab(2 * s + 1)


def _na_attn_body(q_ref, k_ref, v_ref, kc_ref, vc_ref, rpb_ref, o_ref, even_scr, odd_scr, *, rows):
    scale = NA_DH ** -0.5
    kr = min(NA_WIN_ROWS, rows)
    _na_bias_tables(rpb_ref, even_scr, odd_scr)
    kc = kc_ref[0, 0, 0].astype(BF16)
    vc = vc_ref[0, 0, 0].astype(BF16)
    for g in range(rows // NA_GROUP):
        q_rows = range(g * NA_GROUP, (g + 1) * NA_GROUP)
        starts = [min(max(r - kr // 2, 0), rows - kr) for r in q_rows]
        b0, b1 = min(starts), max(starts) + kr
        if (b1 - b0) % 2:
            b0, b1 = (b0, b1 + 1) if b1 < rows else (b0 - 1, b1)
        nb = b1 - b0
        band = slice(b0 * GRID_W, b1 * GRID_W)
        q = q_ref[0, 0, g * NA_GROUP * GRID_W:(g + 1) * NA_GROUP * GRID_W, :]
        s_all = _dot_nt(q, k_ref[0, 0, band, :]) * scale
        col = lax.broadcasted_iota(jnp.int32, (GRID_W, nb * GRID_W), 1)
        strips = []
        for j, (r, rs) in enumerate(zip(q_rows, starts)):
            dr0 = b0 - r + NA_WIN_ROWS - 1
            assert dr0 >= 0 and dr0 + nb <= N_DR - dr0 % 2
            tab = odd_scr if dr0 % 2 else even_scr
            t0 = (dr0 - dr0 % 2) * GRID_W
            s = s_all[j * GRID_W:(j + 1) * GRID_W] + tab[:, t0:t0 + nb * GRID_W]
            lo, hi = (rs - b0) * GRID_W, (rs - b0 + kr) * GRID_W
            if lo > 0 or hi < nb * GRID_W:
                s = jnp.where((col >= lo) & (col < hi), s, NEG_INF)
            strips.append(s)
        s_lat = jnp.concatenate(strips, axis=0)
        s_ctx = _dot_nt(q, kc) * scale
        p_lat, p_ctx = _softmax_rows([s_lat, s_ctx])
        o = _dot(p_lat.astype(BF16), v_ref[0, 0, band, :]) + _dot(p_ctx.astype(BF16), vc)
        o_ref[g * NA_GROUP * GRID_W:(g + 1) * NA_GROUP * GRID_W, :] = o.astype(BF16)


def _na_attn(qkv, cache_k, cache_v, rpb_pad, i_mix):
    nb, _, t, dh = qkv.shape
    past = cache_k.shape[3]
    rows = t // GRID_W
    assert rows % NA_GROUP == 0 and rows >= NA_WIN_ROWS

    def qkv_spec(part):
        return pl.BlockSpec((1, 1, t, dh), lambda b, h: (b, part * NA_HEADS + h, 0, 0))

    cache_spec = pl.BlockSpec((1, 1, 1, past, dh), lambda b, h: (b, i_mix, h, 0, 0))
    return pl.pallas_call(
        functools.partial(_na_attn_body, rows=rows),
        grid=(nb, NA_HEADS),
        in_specs=[qkv_spec(0), qkv_spec(1), qkv_spec(2), cache_spec, cache_spec,
                  pl.BlockSpec((None, 1, N_DR, LANES), lambda b, h: (i_mix, h, 0, 0))],
        out_specs=pl.BlockSpec((t, dh), lambda b, h: (b, h)),
        out_shape=jax.ShapeDtypeStruct((nb * t, NA_HEADS * dh), BF16),
        scratch_shapes=[pltpu.VMEM((GRID_W, N_DR * GRID_W), F32),
                        pltpu.VMEM((GRID_W, (N_DR - 2) * GRID_W), F32)],
        compiler_params=_params(2),
        name="na_attn",
    )(qkv, qkv, qkv, cache_k, cache_v, rpb_pad)


def _rope_tables(t_len):
    t = jnp.arange(t_len)
    n = GLA_DK // 4
    freqs = ROPE_THETA ** (-jnp.arange(n, dtype=F32) / n)

    def tab(pos):
        ang = pos.astype(F32)[:, None] * freqs[None, :]
        cos, sin = jnp.cos(ang), jnp.sin(ang)
        return jnp.concatenate([cos, cos], axis=-1), jnp.concatenate([-sin, sin], axis=-1)

    c_row, s_row = tab(t // GRID_W)
    c_col, s_col = tab(t % GRID_W)
    return jnp.concatenate([c_row, c_col], axis=-1), jnp.concatenate([s_row, s_col], axis=-1)


def _rope(x, cos, sin):
    half = GLA_DK // 2
    swapped = jnp.concatenate(
        [pltpu.roll(x[:, :half], half // 2, axis=1), pltpu.roll(x[:, half:], half // 2, axis=1)],
        axis=1)
    return x * cos + swapped * sin


def _gla_body(*refs, seq, latent):
    if latent:
        (q_ref, k_ref, v_ref, g_ref, lr_ref, wa2_ref, ba_ref, gh_ref, cos_ref, sin_ref, s0_ref,
         o_ref, q_scr, k_scr, la_scr, of_scr, st_scr) = refs
    else:
        (q_ref, k_ref, v_ref, g_ref, lr_ref, wa2_ref, ba_ref, gh_ref,
         o_ref, sf_ref, q_scr, k_scr, la_scr, of_scr) = refs
        assert seq == GLA_BLOCK
    bl, sb = GLA_BLOCK, GLA_CHUNK
    n_blocks = seq // bl

    q = q_ref[...]
    k = k_ref[...]
    if latent:
        q = _rope(q, cos_ref[...], sin_ref[...])
        k = _rope(k, cos_ref[...], sin_ref[...])
    q_scr[...] = q * (GLA_DK ** -0.5)
    k_scr[...] = k
    lr = lr_ref[...].astype(BF16)
    for d in range(2):
        z = _dot(lr, wa2_ref[d].astype(BF16)) + ba_ref[d]
        log_sig = jnp.minimum(z, 0.0) - jnp.log1p(jnp.exp(-jnp.abs(z)))
        la_scr[d] = log_sig * (1.0 / GLA_GATE_NORM)

    r_i = lax.broadcasted_iota(jnp.int32, (bl, bl), 0)
    c_i = lax.broadcasted_iota(jnp.int32, (bl, bl), 1)
    toward_start = [c_i <= r_i, c_i >= r_i]
    gh = gh_ref[...]

    def block(d, rows):
        la = la_scr[d, rows, :]
        hi = la.astype(BF16)
        rem = la - hi.astype(F32)
        mid = rem.astype(BF16)
        lo = (rem - mid.astype(F32)).astype(BF16)
        tri = jnp.where(toward_start[d], 1.0, 0.0).astype(BF16)
        c3 = _dot(tri, jnp.concatenate([hi, mid, lo], axis=1))
        cum = (c3[:, :GLA_DK] + c3[:, GLA_DK:2 * GLA_DK]) + c3[:, 2 * GLA_DK:]
        tot = cum[bl - 1:bl] if d == 0 else cum[0:1]
        qq = q_scr[rows, :]
        kk = k_scr[rows, :]
        v = v_ref[rows, :].astype(BF16)

        a_parts = []
        for i in range(bl // sb):
            r0, r1 = i * sb, (i + 1) * sb
            if d == 0:
                ref = cum[r0 - 1:r0] if i > 0 else 0.0
                n0, n1 = 0, r1
            else:
                ref = cum[r1:r1 + 1] if r1 < bl else 0.0
                n0, n1 = r0, bl
            qd = (qq[r0:r1] * jnp.exp(cum[r0:r1] - ref)).astype(BF16)
            ks = (kk[n0:n1] * jnp.exp(ref - cum[n0:n1])).astype(BF16)
            pieces = [ks]
            if n0 > 0:
                pieces.insert(0, jnp.zeros((n0, GLA_DK), BF16))
            if n1 < bl:
                pieces.append(jnp.zeros((bl - n1, GLA_DK), BF16))
            a = _dot_nt(qd, jnp.concatenate(pieces, axis=0) if len(pieces) > 1 else ks)
            a_parts.append(jnp.where(toward_start[d][r0:r1], a, 0.0).astype(BF16))
        attn = jnp.concatenate(a_parts, axis=0)
        o = _dot(attn, v)
        k2 = (kk * jnp.exp(tot - cum)).astype(BF16)
        if not latent:
            return o, _dot_tn(k2, v)
        st = st_scr[...]
        o = o + _dot_nt((qq * jnp.exp(cum)).astype(BF16), st.astype(BF16))
        st_scr[...] = st * jnp.exp(tot) + _dot_tn(v, k2)
        return o, None

    def finish(rows, o):
        o = o * lax.rsqrt(jnp.mean(o * o, axis=-1, keepdims=True) + EPS) * gh
        g = g_ref[rows, :]
        o_ref[rows, :] = (o * (g * jax.nn.sigmoid(g))).astype(BF16)

    if not latent:
        rows = slice(0, bl)
        o_f, s_f = block(0, rows)
        sf_ref[0, 0, 0] = s_f
        of_scr[...] = o_f
        o_b, s_b = block(1, rows)
        sf_ref[0, 1, 0] = s_b
        finish(rows, of_scr[...] + o_b)
        return

    def rows_of(bi):
        return pl.ds(pl.multiple_of(bi * bl, bl), bl)

    st_scr[...] = s0_ref[0, 0, 0, 0].T

    def fwd(bi, carry):
        rows = rows_of(bi)
        of_scr[rows, :] = block(0, rows)[0]
        return carry

    lax.fori_loop(0, n_blocks, fwd, 0)
    st_scr[...] = s0_ref[0, 0, 1, 0].T

    def bwd(j, carry):
        rows = rows_of(n_blocks - 1 - j)
        finish(rows, of_scr[rows, :] + block(1, rows)[0])
        return carry

    lax.fori_loop(0, n_blocks, bwd, 0)


def _gla(proj, lr, w_a2_pad, b_a, g_head, i_mix, *, seq, n_mix, rope=None, s0=None, prev_s=None):
    m = proj.shape[0]
    nb = m // seq
    latent = s0 is not None
    kq = GLA_KEY_DIM // GLA_DK
    kv = 2 * GLA_KEY_DIM // GLA_DV
    in_specs = [
        pl.BlockSpec((seq, GLA_DK), lambda b, h: (b, h)),
        pl.BlockSpec((seq, GLA_DK), lambda b, h: (b, kq + h)),
        pl.BlockSpec((seq, GLA_DV), lambda b, h: (b, kv + h)),
        pl.BlockSpec((seq, GLA_DV), lambda b, h: (b, kv + GLA_HEADS + h)),
        pl.BlockSpec((seq, LANES), lambda b, h: (b, 0)),
        pl.BlockSpec((None, 2, LANES, GLA_DK), lambda b, h: (i_mix, 0, 0, h)),
        pl.BlockSpec((None, 2, 1, GLA_DK), lambda b, h: (i_mix, 0, 0, h)),
        pl.BlockSpec((None, 1, GLA_DV), lambda b, h: (i_mix, 0, 0)),
    ]
    args = [proj, proj, proj, proj, lr, w_a2_pad, b_a, g_head]
    a_spec = pl.BlockSpec((seq, GLA_DV), lambda b, h: (b, h))
    a_shape = jax.ShapeDtypeStruct((m, GLA_HEADS * GLA_DV), BF16)
    scratch = [pltpu.VMEM((seq, GLA_DK), F32), pltpu.VMEM((seq, GLA_DK), F32),
               pltpu.VMEM((2, seq, GLA_DK), F32), pltpu.VMEM((seq, GLA_DV), F32)]
    aliases = {}
    if latent:
        in_specs += [
            pl.BlockSpec((seq, GLA_DK), lambda b, h: (0, 0)),
            pl.BlockSpec((seq, GLA_DK), lambda b, h: (0, 0)),
            pl.BlockSpec((1, 1, 2, 1, GLA_DK, GLA_DV), lambda b, h: (b, i_mix, 0, h, 0, 0)),
        ]
        args += [rope[0], rope[1], s0]
        out_specs, out_shape = a_spec, a_shape
        scratch.append(pltpu.VMEM((GLA_DV, GLA_DK), F32))
    else:
        out_specs = [a_spec, pl.BlockSpec((1, None, 2, 1, GLA_DK, GLA_DV),
                                          lambda b, h: (b, i_mix, 0, h, 0, 0))]
        out_shape = [a_shape,
                     jax.ShapeDtypeStruct((nb, n_mix, 2, GLA_HEADS, GLA_DK, GLA_DV), F32)]
        if prev_s is not None:
            in_specs.append(pl.BlockSpec(memory_space=pl.ANY))
            args.append(prev_s)
            aliases = {len(args) - 1: 1}
    body = functools.partial(_gla_body, seq=seq, latent=latent)
    if not latent and prev_s is not None:
        body = _drop_carried(body, n_in=len(args))
    return pl.pallas_call(
        body,
        grid=(nb, GLA_HEADS),
        in_specs=in_specs,
        out_specs=out_specs,
        out_shape=out_shape,
        scratch_shapes=scratch,
        input_output_aliases=aliases,
        compiler_params=_params(2),
        name="gla_sample" if latent else "gla_prompt",
    )(*args)


def _drop_carried(body, n_in):
    def wrapped(*refs):
        return body(*refs[:n_in - 1], *refs[n_in:])
    return wrapped


def kernel(x_prompt, x_sample, cache_na_k, cache_na_v, state_gla, c, c_ctx, w_ada, b_ada, norm_g,
           w_qkv_na, w_o_na, rpb_na, w_qkvg_gla, w_a1_gla, w_a2_gla, b_a_gla, g_head_gla, w_o_gla,
           w1_mlp, w2_mlp):
    n_p, seq_p, _ = x_prompt.shape
    n_s, seq_s, _ = x_sample.shape
    assert seq_s == TM and TM % seq_p == 0 and 1 + n_s <= MOD_ROWS
    xp = x_prompt.reshape(n_p * seq_p, D_MODEL)
    xs = x_sample.reshape(n_s * seq_s, D_MODEL)

    cond = jnp.zeros((MOD_ROWS, D_MODEL), F32).at[0].set(c_ctx).at[1:1 + n_s].set(c)
    mods = _ada_all(cond, w_ada, b_ada)
    norm_g4 = norm_g.reshape(DEPTH, 4, 1, D_MODEL)

    n_na = w_qkv_na.shape[0]
    rpb_pad = jnp.pad(rpb_na, ((0, 0), (0, 0), (0, N_DR - rpb_na.shape[2]),
                               (0, LANES - rpb_na.shape[3])))

    n_gla = w_a1_gla.shape[0]
    w_a1_pad = jnp.zeros((n_gla, D_MODEL, LANES), F32).at[:, :, :2 * GLA_RANK].set(
        w_a1_gla.transpose(0, 2, 1, 3).reshape(n_gla, D_MODEL, 2 * GLA_RANK))
    w_a2_pad = jnp.zeros((n_gla, 2, LANES, GLA_KEY_DIM), F32)
    for d in range(2):
        w_a2_pad = w_a2_pad.at[:, d, d * GLA_RANK:(d + 1) * GLA_RANK].set(w_a2_gla[:, d])
    b_a4 = b_a_gla.reshape(n_gla, 2, 1, GLA_KEY_DIM)
    g_head3 = g_head_gla.reshape(n_gla, 1, GLA_DV)
    rope = _rope_tables(seq_s)

    new_kv, new_s = (), None
    for l in range(DEPTH):
        i = l // N_MIXERS
        if l % N_MIXERS == 0:
            q_p, k_new, v_new = _qkv_prompt(xp, norm_g4, mods, w_qkv_na, l, i, seq_p, n_na, new_kv)
            new_kv = (k_new, v_new)
            a_p = _ctx_attn(q_p, k_new, v_new, i)
            qkv_s = _qkv_sample(xs, norm_g4, mods, w_qkv_na, l, i, seq_s)
            a_s = _na_attn(qkv_s, cache_na_k, cache_na_v, rpb_pad, i)
            w_o = w_o_na
        else:
            proj_p, lr_p = _gla_proj(xp, norm_g4, mods, w_qkvg_gla, w_a1_pad[i], l, i,
                                     row0=0, row_stride=0, name="gla_proj_prompt")
            a_p, new_s = _gla(proj_p, lr_p, w_a2_pad, b_a4, g_head3, i, seq=seq_p, n_mix=n_gla,
                              prev_s=new_s)
            proj_s, lr_s = _gla_proj(xs, norm_g4, mods, w_qkvg_gla, w_a1_pad[i], l, i,
                                     row0=1, row_stride=TM // seq_s, name="gla_proj_sample")
            a_s = _gla(proj_s, lr_s, w_a2_pad, b_a4, g_head3, i, seq=seq_s, n_mix=n_gla,
                       rope=rope, s0=state_gla)
            w_o = w_o_gla
        xp = _oproj(a_p, w_o, i, xp, norm_g4, mods, l, row0=0, n_groups=1, name="oproj_prompt")
        xs = _oproj(a_s, w_o, i, xs, norm_g4, mods, l, row0=1, n_groups=n_s, name="oproj_sample")
        xp = _mlp(xp, norm_g4, mods, w1_mlp, w2_mlp, l, row0=0, row_stride=0, name="mlp_prompt")
        xs = _mlp(xs, norm_g4, mods, w1_mlp, w2_mlp, l, row0=1, row_stride=TM // seq_s,
                  name="mlp_sample")

    return (xp.reshape(n_p, seq_p, D_MODEL), xs.reshape(n_s, seq_s, D_MODEL),
            new_kv[0], new_kv[1], new_s)
```

```python
import functools

import jax
import jax.numpy as jnp
from jax import lax
from jax.experimental import pallas as pl
from jax.experimental.pallas import tpu as pltpu

F32 = jnp.float32
BF16 = jnp.bfloat16

D_MODEL = 2048
DEPTH = 4
N_MIXERS = 2
GRID_W = 64
NA_HEADS = 16
NA_DH = 128
NA_WIN_ROWS = 8
NA_WIN_COLS = 16
GLA_HEADS = 4
GLA_DK = 256
GLA_DV = 512
GLA_KEY_DIM = GLA_HEADS * GLA_DK
GLA_RANK = 16
GLA_GATE_NORM = 16.0
GLA_CHUNK = 64
D_FF = 4 * D_MODEL
ROPE_THETA = 10000.0
EPS = 1e-6
NEG_INF = -1e30

MOD_ROWS = 8
LANES = 128
TM = 1024
TN = 512
TF = 512
TM_OUT = 512
ROW_CHUNK = 256
NA_GROUP = 4
GLA_BLOCK = 256
VMEM_LIMIT = 58 * 1024 * 1024


def _params(n_axes):
    return pltpu.CompilerParams(
        dimension_semantics=("arbitrary",) * n_axes, vmem_limit_bytes=VMEM_LIMIT)


def _dot(a, b):
    return jnp.dot(a, b, preferred_element_type=F32)


def _dot_nt(a, b):
    return lax.dot_general(a, b, (((1,), (1,)), ((), ())), preferred_element_type=F32)


def _dot_tn(a, b):
    return lax.dot_general(a, b, (((0,), (0,)), ((), ())), preferred_element_type=F32)


def _ada_body(c_ref, w_ref, b_ref, o_ref):
    c = c_ref[...]
    s = c * jax.nn.sigmoid(c)
    o_ref[0, 0] = _dot(s.astype(BF16), w_ref[0].astype(BF16)) + b_ref[0, 0]


def _ada_all(cond, w_ada, b_ada):
    return pl.pallas_call(
        _ada_body,
        grid=(DEPTH, 6),
        in_specs=[
            pl.BlockSpec((MOD_ROWS, D_MODEL), lambda l, j: (0, 0)),
            pl.BlockSpec((1, D_MODEL, D_MODEL), lambda l, j: (l, 0, j)),
            pl.BlockSpec((1, 1, 1, D_MODEL), lambda l, j: (l, j, 0, 0)),
        ],
        out_specs=pl.BlockSpec((1, 1, MOD_ROWS, D_MODEL), lambda l, j: (l, j, 0, 0)),
        out_shape=jax.ShapeDtypeStruct((DEPTH, 6, MOD_ROWS, D_MODEL), F32),
        compiler_params=_params(2),
        name="adaln",
    )(cond, w_ada, b_ada.reshape(DEPTH, 6, 1, D_MODEL))


def _mod_row(ref, row):
    return ref[0, 0, pl.ds(row, 1), :]


def _row_chunks(n_rows):
    return [slice(r, r + ROW_CHUNK) for r in range(0, n_rows, ROW_CHUNK)]


def _mod_gain(g_ref, scale_ref, row):
    return g_ref[0, 0] * (1.0 + _mod_row(scale_ref, row))


def _norm_mod(x, gain, shift):
    y = x * lax.rsqrt(jnp.mean(x * x, axis=-1, keepdims=True) + EPS)
    return (y * gain + shift).astype(BF16)


def _gated_residual(x, y, gain):
    return x + (y * lax.rsqrt(jnp.mean(y * y, axis=-1, keepdims=True) + EPS)) * gain


def _mod_spec(layer, j):
    return pl.BlockSpec((1, 1, MOD_ROWS, D_MODEL), lambda *_: (layer, j, 0, 0))


def _gain_spec(layer, j):
    return pl.BlockSpec((1, 1, 1, D_MODEL), lambda *_: (layer, j, 0, 0))


def _proj_body(x_ref, g_ref, shift_ref, scale_ref, w_ref, *rest, row0, row_stride, n_extra,
               lowrank, write):
    extra = rest[:n_extra]
    outs = rest[n_extra:-1]
    h_scr = rest[-1]
    i = pl.program_id(0)
    n = pl.program_id(1)

    @pl.when(n == 0)
    def _():
        w = w_ref[...].astype(BF16)
        row = row0 + i * row_stride
        gain = _mod_gain(g_ref, scale_ref, row)
        shift = _mod_row(shift_ref, row)
        for rows in _row_chunks(TM):
            h = _norm_mod(x_ref[rows, :], gain, shift)
            h_scr[rows, :] = h
            write(lambda: _dot(h, w), outs, 0, rows.start)
            if lowrank:
                outs[-1][rows, :] = _dot(h, extra[0][...].astype(BF16))

    @pl.when(n > 0)
    def _():
        write(lambda: _dot(h_scr[...], w_ref[...].astype(BF16)), outs, n, 0)


def _write_heads(o_ref, res, seq, r0):
    n_rows = res.shape[0]
    for j in range(o_ref.shape[1]):
        cols = slice(j * NA_DH, (j + 1) * NA_DH)
        if n_rows >= seq:
            for b in range(n_rows // seq):
                o_ref[r0 // seq + b, j] = res[b * seq:(b + 1) * seq, cols].astype(o_ref.dtype)
        else:
            l0 = r0 % seq
            o_ref[r0 // seq, j, l0:l0 + n_rows, :] = res[:, cols].astype(o_ref.dtype)


def _write_qkv_prompt(res, outs, n, r0, *, seq):
    q_ref, k_ref, v_ref = outs
    nq = D_MODEL // TN
    if isinstance(n, int):
        _write_heads(outs[n // nq], res(), seq, r0)
        return

    @pl.when(n < nq)
    def _():
        _write_heads(q_ref, res(), seq, r0)

    @pl.when(jnp.logical_and(n >= nq, n < 2 * nq))
    def _():
        _write_heads(k_ref, res(), seq, r0)

    @pl.when(n >= 2 * nq)
    def _():
        _write_heads(v_ref, res(), seq, r0)


def _write_qkv_sample(res, outs, n, r0, *, seq):
    _write_heads(outs[0], res(), seq, r0)


def _write_plain(res, outs, n, r0):
    val = res()
    outs[0][r0:r0 + val.shape[0], :] = val


def _proj_call(x, norm_g, mods, w, layer, w_layer, *, row0, row_stride, write, out_specs,
               out_shape, lowrank_w=None, carried=(), aliases=None, name):
    m = x.shape[0]
    n_cols = w.shape[-1]
    in_specs = [
        pl.BlockSpec((TM, D_MODEL), lambda i, n: (i, 0)),
        _gain_spec(layer, 0),
        _mod_spec(layer, 0),
        _mod_spec(layer, 1),
        pl.BlockSpec((None, D_MODEL, TN), lambda i, n: (w_layer, 0, n)),
    ]
    args = [x, norm_g, mods, mods, w]
    if lowrank_w is not None:
        in_specs.append(pl.BlockSpec(lowrank_w.shape, lambda i, n: (0, 0)))
        args.append(lowrank_w)
    for arr in carried:
        in_specs.append(pl.BlockSpec(memory_space=pl.ANY))
        args.append(arr)
    body = functools.partial(_proj_body, row0=row0, row_stride=row_stride,
                             n_extra=len(args) - 5, lowrank=lowrank_w is not None, write=write)
    return pl.pallas_call(
        body,
        grid=(m // TM, n_cols // TN),
        in_specs=in_specs,
        out_specs=out_specs,
        out_shape=out_shape,
        scratch_shapes=[pltpu.VMEM((TM, D_MODEL), BF16)],
        input_output_aliases=aliases or {},
        compiler_params=_params(2),
        name=name,
    )(*args)


def _qkv_prompt(x, norm_g, mods, w_qkv, layer, i_mix, seq, n_mix, prev_kv):
    nb = x.shape[0] // seq
    tb = TM // seq
    th = TN // NA_DH
    nq = D_MODEL // TN
    q_shape = (nb, NA_HEADS, seq, NA_DH)
    kv_shape = (nb, n_mix, NA_HEADS, seq, NA_DH)
    q_blk = (tb, th, seq, NA_DH)
    kv_blk = (tb, None, th, seq, NA_DH)
    return _proj_call(
        x, norm_g, mods, w_qkv, layer, i_mix, row0=0, row_stride=0,
        write=functools.partial(_write_qkv_prompt, seq=seq),
        out_specs=[
            pl.BlockSpec(q_blk, lambda i, n: (i, jnp.minimum(n, nq - 1), 0, 0)),
            pl.BlockSpec(kv_blk, lambda i, n: (i, i_mix, jnp.clip(n - nq, 0, nq - 1), 0, 0)),
            pl.BlockSpec(kv_blk, lambda i, n: (i, i_mix, jnp.clip(n - 2 * nq, 0, nq - 1), 0, 0)),
        ],
        out_shape=[jax.ShapeDtypeStruct(q_shape, BF16), jax.ShapeDtypeStruct(kv_shape, F32),
                   jax.ShapeDtypeStruct(kv_shape, F32)],
        carried=prev_kv, aliases={5: 1, 6: 2} if prev_kv else None,
        name="qkv_prompt")


def _qkv_sample(x, norm_g, mods, w_qkv, layer, i_mix, seq):
    nb = x.shape[0] // seq
    return _proj_call(
        x, norm_g, mods, w_qkv, layer, i_mix, row0=1, row_stride=TM // seq,
        write=functools.partial(_write_qkv_sample, seq=seq),
        out_specs=[pl.BlockSpec((TM // seq, TN // NA_DH, seq, NA_DH), lambda i, n: (i, n, 0, 0))],
        out_shape=[jax.ShapeDtypeStruct((nb, 3 * NA_HEADS, seq, NA_DH), BF16)],
        name="qkv_sample")[0]


def _gla_proj(x, norm_g, mods, w_qkvg, w_a1_pad, layer, i_mix, *, row0, row_stride, name):
    m = x.shape[0]
    n_cols = w_qkvg.shape[-1]
    return _proj_call(
        x, norm_g, mods, w_qkvg, layer, i_mix, row0=row0, row_stride=row_stride,
        write=_write_plain,
        out_specs=[pl.BlockSpec((TM, TN), lambda i, n: (i, n)),
                   pl.BlockSpec((TM, LANES), lambda i, n: (i, 0))],
        out_shape=[jax.ShapeDtypeStruct((m, n_cols), F32), jax.ShapeDtypeStruct((m, LANES), F32)],
        lowrank_w=w_a1_pad, name=name)


def _oproj_body(a_ref, w_ref, x_ref, gn_ref, gate_ref, o_ref, w_scr, *, row0, tiles_per_group):
    i = pl.program_id(0)

    @pl.when(i == 0)
    def _():
        w_scr[...] = w_ref[...].astype(BF16)

    gain = gn_ref[0, 0] * _mod_row(gate_ref, row0 + i // tiles_per_group)
    for rows in _row_chunks(TM_OUT):
        o_ref[rows, :] = _gated_residual(x_ref[rows, :], _dot(a_ref[rows, :], w_scr[...]), gain)


def _oproj(a, w, w_layer, x, norm_g, mods, layer, *, row0, n_groups, name):
    m = x.shape[0]
    tiles_per_group = m // n_groups // TM_OUT
    body = functools.partial(_oproj_body, row0=row0, tiles_per_group=tiles_per_group)
    return pl.pallas_call(
        body,
        grid=(m // TM_OUT,),
        in_specs=[
            pl.BlockSpec((TM_OUT, D_MODEL), lambda i: (i, 0)),
            pl.BlockSpec((None, D_MODEL, D_MODEL), lambda i: (w_layer, 0, 0),
                         pipeline_mode=pl.Buffered(1)),
            pl.BlockSpec((TM_OUT, D_MODEL), lambda i: (i, 0)),
            _gain_spec(layer, 1),
            _mod_spec(layer, 2),
        ],
        out_specs=pl.BlockSpec((TM_OUT, D_MODEL), lambda i: (i, 0)),
        out_shape=jax.ShapeDtypeStruct((m, D_MODEL), F32),
        scratch_shapes=[pltpu.VMEM((D_MODEL, D_MODEL), BF16)],
        compiler_params=_params(1),
        name=name,
    )(a, w, x, norm_g, mods)


def _mlp_body(x_ref, g_ref, shift_ref, scale_ref, gn_ref, gate_ref, w1_ref, w2_ref, o_ref, h_scr,
              a_scr, *, row0, row_stride):
    i = pl.program_id(0)
    f = pl.program_id(1)
    last = pl.num_programs(1) - 1
    row = row0 + i * row_stride

    def act(a):
        return jnp.square(jnp.maximum(a, 0.0)).astype(BF16)

    @pl.when(f == 0)
    def _():
        w1 = w1_ref[...].astype(BF16)
        gain = _mod_gain(g_ref, scale_ref, row)
        shift = _mod_row(shift_ref, row)
        for rows in _row_chunks(TM):
            h = _norm_mod(x_ref[rows, :], gain, shift)
            h_scr[rows, :] = h
            a_scr[rows, :] = act(_dot(h, w1))
        o_ref[...] = _dot(a_scr[...], w2_ref[...].astype(BF16))

    @pl.when(jnp.logical_and(f > 0, f < last))
    def _():
        a = act(_dot(h_scr[...], w1_ref[...].astype(BF16)))
        o_ref[...] += _dot(a, w2_ref[...].astype(BF16))

    @pl.when(f == last)
    def _():
        a_scr[...] = act(_dot(h_scr[...], w1_ref[...].astype(BF16)))
        w2 = w2_ref[...].astype(BF16)
        gain = gn_ref[0, 0] * _mod_row(gate_ref, row)
        for rows in _row_chunks(TM):
            y = o_ref[rows, :] + _dot(a_scr[rows, :], w2)
            o_ref[rows, :] = _gated_residual(x_ref[rows, :], y, gain)


def _mlp(x, norm_g, mods, w1, w2, layer, *, row0, row_stride, name):
    m = x.shape[0]
    body = functools.partial(_mlp_body, row0=row0, row_stride=row_stride)
    return pl.pallas_call(
        body,
        grid=(m // TM, D_FF // TF),
        in_specs=[
            pl.BlockSpec((TM, D_MODEL), lambda i, f: (i, 0), pipeline_mode=pl.Buffered(1)),
            _gain_spec(layer, 2),
            _mod_spec(layer, 3),
            _mod_spec(layer, 4),
            _gain_spec(layer, 3),
            _mod_spec(layer, 5),
            pl.BlockSpec((None, D_MODEL, TF), lambda i, f: (layer, 0, f)),
            pl.BlockSpec((None, TF, D_MODEL), lambda i, f: (layer, f, 0)),
        ],
        out_specs=pl.BlockSpec((TM, D_MODEL), lambda i, f: (i, 0)),
        out_shape=jax.ShapeDtypeStruct((m, D_MODEL), F32),
        scratch_shapes=[pltpu.VMEM((TM, D_MODEL), BF16), pltpu.VMEM((TM, TF), BF16)],
        compiler_params=_params(2),
        name=name,
    )(x, norm_g, mods, mods, norm_g, mods, w1, w2)


def _softmax_rows(parts):
    m = parts[0].max(axis=-1, keepdims=True)
    for s in parts[1:]:
        m = jnp.maximum(m, s.max(axis=-1, keepdims=True))
    es = [jnp.exp(s - m) for s in parts]
    den = es[0].sum(axis=-1, keepdims=True)
    for e in es[1:]:
        den = den + e.sum(axis=-1, keepdims=True)
    return [e / den for e in es]


def _ctx_attn_body(q_ref, k_ref, v_ref, o_ref):
    scale = NA_DH ** -0.5
    for h in range(NA_HEADS):
        q = q_ref[0, h]
        k = k_ref[0, h].astype(BF16)
        v = v_ref[0, h].astype(BF16)
        (p,) = _softmax_rows([_dot_nt(q, k) * scale])
        o_ref[:, h * NA_DH:(h + 1) * NA_DH] = _dot(p.astype(BF16), v).astype(BF16)


def _ctx_attn(q, k, v, i_mix):
    nb, nh, seq, dh = q.shape
    kv_spec = pl.BlockSpec((1, None, nh, seq, dh), lambda b: (b, i_mix, 0, 0, 0))
    return pl.pallas_call(
        _ctx_attn_body,
        grid=(nb,),
        in_specs=[pl.BlockSpec((1, nh, seq, dh), lambda b: (b, 0, 0, 0)), kv_spec, kv_spec],
        out_specs=pl.BlockSpec((seq, nh * dh), lambda b: (b, 0)),
        out_shape=jax.ShapeDtypeStruct((nb * seq, nh * dh), BF16),
        compiler_params=_params(1),
        name="ctx_attn",
    )(q, k, v)


N_DR = 2 * NA_WIN_ROWS


def _na_bias_tables(rpb_ref, even_scr, odd_scr):
    assert LANES == 2 * GRID_W
    qc = lax.broadcasted_iota(jnp.int32, (GRID_W, LANES), 0)
    lane = lax.broadcasted_iota(jnp.int32, (GRID_W, LANES), 1)
    kc = lane & (GRID_W - 1)
    col_start = jnp.clip(qc - NA_WIN_COLS // 2, 0, GRID_W - NA_WIN_COLS)
    valid = (kc >= col_start) & (kc < col_start + NA_WIN_COLS)
    first = lane < GRID_W

    def toeplitz(dr, lane_off):
        g = jnp.broadcast_to(rpb_ref[0, dr:dr + 1, :], (GRID_W, LANES))
        shift = (LANES - (NA_WIN_COLS - 1) + lane_off) % LANES
        return pltpu.roll(g, shift, 1, stride=1, stride_axis=0)

    def slab(dr):
        t = jnp.where(first, toeplitz(dr, 0), toeplitz(dr + 1, GRID_W))
        return jnp.where(valid, t, NEG_INF)

    for s in range(N_DR // 2):
        even_scr[:, s * LANES:(s + 1) * LANES] = slab(2 * s)
    for s in range(N_DR // 2 - 1):
        odd_scr[:, s * LANES:(s + 1) * LANES] = slab(2 * s + 1)


def _na_attn_body(q_ref, k_ref, v_ref, kc_ref, vc_ref, rpb_ref, o_ref, even_scr, odd_scr, *, rows):
    scale = NA_DH ** -0.5
    kr = min(NA_WIN_ROWS, rows)
    _na_bias_tables(rpb_ref, even_scr, odd_scr)
    kc = kc_ref[0, 0, 0].astype(BF16)
    vc = vc_ref[0, 0, 0].astype(BF16)
    for g in range(rows // NA_GROUP):
        q_rows = range(g * NA_GROUP, (g + 1) * NA_GROUP)
        starts = [min(max(r - kr // 2, 0), rows - kr) for r in q_rows]
        b0, b1 = min(starts), max(starts) + kr
        if (b1 - b0) % 2:
            b0, b1 = (b0, b1 + 1) if b1 < rows else (b0 - 1, b1)
        nb = b1 - b0
        band = slice(b0 * GRID_W, b1 * GRID_W)
        q = q_ref[0, 0, g * NA_GROUP * GRID_W:(g + 1) * NA_GROUP * GRID_W, :]
        s_all = _dot_nt(q, k_ref[0, 0, band, :]) * scale
        col = lax.broadcasted_iota(jnp.int32, (GRID_W, nb * GRID_W), 1)
        strips = []
        for j, (r, rs) in enumerate(zip(q_rows, starts)):
            dr0 = b0 - r + NA_WIN_ROWS - 1
            assert dr0 >= 0 and dr0 + nb <= N_DR - dr0 % 2
            tab = odd_scr if dr0 % 2 else even_scr
            t0 = (dr0 - dr0 % 2) * GRID_W
            s = s_all[j * GRID_W:(j + 1) * GRID_W] + tab[:, t0:t0 + nb * GRID_W]
            lo, hi = (rs - b0) * GRID_W, (rs - b0 + kr) * GRID_W
            if lo > 0 or hi < nb * GRID_W:
                s = jnp.where((col >= lo) & (col < hi), s, NEG_INF)
            strips.append(s)
        s_lat = jnp.concatenate(strips, axis=0)
        s_ctx = _dot_nt(q, kc) * scale
        p_lat, p_ctx = _softmax_rows([s_lat, s_ctx])
        o = _dot(p_lat.astype(BF16), v_ref[0, 0, band, :]) + _dot(p_ctx.astype(BF16), vc)
        o_ref[g * NA_GROUP * GRID_W:(g + 1) * NA_GROUP * GRID_W, :] = o.astype(BF16)


def _na_attn(qkv, cache_k, cache_v, rpb_pad, i_mix):
    nb, _, t, dh = qkv.shape
    past = cache_k.shape[3]
    rows = t // GRID_W
    assert rows % NA_GROUP == 0 and rows >= NA_WIN_ROWS

    def qkv_spec(part):
        return pl.BlockSpec((1, 1, t, dh), lambda b, h: (b, part * NA_HEADS + h, 0, 0))

    cache_spec = pl.BlockSpec((1, 1, 1, past, dh), lambda b, h: (b, i_mix, h, 0, 0))
    return pl.pallas_call(
        functools.partial(_na_attn_body, rows=rows),
        grid=(nb, NA_HEADS),
        in_specs=[qkv_spec(0), qkv_spec(1), qkv_spec(2), cache_spec, cache_spec,
                  pl.BlockSpec((None, 1, N_DR, LANES), lambda b, h: (i_mix, h, 0, 0))],
        out_specs=pl.BlockSpec((t, dh), lambda b, h: (b, h)),
        out_shape=jax.ShapeDtypeStruct((nb * t, NA_HEADS * dh), BF16),
        scratch_shapes=[pltpu.VMEM((GRID_W, N_DR * GRID_W), F32),
                        pltpu.VMEM((GRID_W, (N_DR - 2) * GRID_W), F32)],
        compiler_params=_params(2),
        name="na_attn",
    )(qkv, qkv, qkv, cache_k, cache_v, rpb_pad)


def _rope_tables(t_len):
    t = jnp.arange(t_len)
    n = GLA_DK // 4
    freqs = ROPE_THETA ** (-jnp.arange(n, dtype=F32) / n)

    def tab(pos):
        ang = pos.astype(F32)[:, None] * freqs[None, :]
        cos, sin = jnp.cos(ang), jnp.sin(ang)
        return jnp.concatenate([cos, cos], axis=-1), jnp.concatenate([-sin, sin], axis=-1)

    c_row, s_row = tab(t // GRID_W)
    c_col, s_col = tab(t % GRID_W)
    return jnp.concatenate([c_row, c_col], axis=-1), jnp.concatenate([s_row, s_col], axis=-1)


def _rope(x, cos, sin):
    half = GLA_DK // 2
    swapped = jnp.concatenate(
        [pltpu.roll(x[:, :half], half // 2, axis=1), pltpu.roll(x[:, half:], half // 2, axis=1)],
        axis=1)
    return x * cos + swapped * sin


def _gla_body(*refs, seq, latent):
    if latent:
        (q_ref, k_ref, v_ref, g_ref, lr_ref, wa2_ref, ba_ref, gh_ref, cos_ref, sin_ref, s0_ref,
         o_ref, q_scr, k_scr, la_scr, of_scr, st_scr) = refs
    else:
        (q_ref, k_ref, v_ref, g_ref, lr_ref, wa2_ref, ba_ref, gh_ref,
         o_ref, sf_ref, q_scr, k_scr, la_scr, of_scr) = refs
        assert seq == GLA_BLOCK
    bl, sb = GLA_BLOCK, GLA_CHUNK
    n_blocks = seq // bl

    q = q_ref[...]
    k = k_ref[...]
    if latent:
        q = _rope(q, cos_ref[...], sin_ref[...])
        k = _rope(k, cos_ref[...], sin_ref[...])
    q_scr[...] = q * (GLA_DK ** -0.5)
    k_scr[...] = k
    lr = lr_ref[...].astype(BF16)
    for d in range(2):
        z = _dot(lr, wa2_ref[d].astype(BF16)) + ba_ref[d]
        log_sig = jnp.minimum(z, 0.0) - jnp.log(1.0 + jnp.exp(-jnp.abs(z)))
        la_scr[d] = log_sig * (1.0 / GLA_GATE_NORM)

    r_i = lax.broadcasted_iota(jnp.int32, (bl, bl), 0)
    c_i = lax.broadcasted_iota(jnp.int32, (bl, bl), 1)
    toward_start = [c_i <= r_i, c_i >= r_i]
    gh = gh_ref[...]

    def block(d, rows):
        la = la_scr[d, rows, :]
        hi = la.astype(BF16)
        rem = la - hi.astype(F32)
        mid = rem.astype(BF16)
        lo = (rem - mid.astype(F32)).astype(BF16)
        tri = jnp.where(toward_start[d], 1.0, 0.0).astype(BF16)
        c3 = _dot(tri, jnp.concatenate([hi, mid, lo], axis=1))
        cum = (c3[:, :GLA_DK] + c3[:, GLA_DK:2 * GLA_DK]) + c3[:, 2 * GLA_DK:]
        tot = cum[bl - 1:bl] if d == 0 else cum[0:1]
        qq = q_scr[rows, :]
        kk = k_scr[rows, :]
        v = v_ref[rows, :].astype(BF16)

        a_parts = []
        for i in range(bl // sb):
            r0, r1 = i * sb, (i + 1) * sb
            if d == 0:
                ref = cum[r0 - 1:r0] if i > 0 else 0.0
                n0, n1 = 0, r1
            else:
                ref = cum[r1:r1 + 1] if r1 < bl else 0.0
                n0, n1 = r0, bl
            qd = (qq[r0:r1] * jnp.exp(cum[r0:r1] - ref)).astype(BF16)
            ks = (kk[n0:n1] * jnp.exp(ref - cum[n0:n1])).astype(BF16)
            pieces = [ks]
            if n0 > 0:
                pieces.insert(0, jnp.zeros((n0, GLA_DK), BF16))
            if n1 < bl:
                pieces.append(jnp.zeros((bl - n1, GLA_DK), BF16))
            a = _dot_nt(qd, jnp.concatenate(pieces, axis=0) if len(pieces) > 1 else ks)
            a_parts.append(jnp.where(toward_start[d][r0:r1], a, 0.0).astype(BF16))
        attn = jnp.concatenate(a_parts, axis=0)
        o = _dot(attn, v)
        k2 = (kk * jnp.exp(tot - cum)).astype(BF16)
        if not latent:
            return o, _dot_tn(k2, v)
        st = st_scr[...]
        o = o + _dot_nt((qq * jnp.exp(cum)).astype(BF16), st.astype(BF16))
        st_scr[...] = st * jnp.exp(tot) + _dot_tn(v, k2)
        return o, None

    def finish(rows, o):
        o = o * lax.rsqrt(jnp.mean(o * o, axis=-1, keepdims=True) + EPS) * gh
        g = g_ref[rows, :]
        o_ref[rows, :] = (o * (g * jax.nn.sigmoid(g))).astype(BF16)

    if not latent:
        rows = slice(0, bl)
        o_f, s_f = block(0, rows)
        sf_ref[0, 0, 0] = s_f
        of_scr[...] = o_f
        o_b, s_b = block(1, rows)
        sf_ref[0, 1, 0] = s_b
        finish(rows, of_scr[...] + o_b)
        return

    def rows_of(bi):
        return pl.ds(pl.multiple_of(bi * bl, bl), bl)

    st_scr[...] = s0_ref[0, 0, 0, 0].T

    def fwd(bi, carry):
        rows = rows_of(bi)
        of_scr[rows, :] = block(0, rows)[0]
        return carry

    lax.fori_loop(0, n_blocks, fwd, 0)
    st_scr[...] = s0_ref[0, 0, 1, 0].T

    def bwd(j, carry):
        rows = rows_of(n_blocks - 1 - j)
        finish(rows, of_scr[rows, :] + block(1, rows)[0])
        return carry

    lax.fori_loop(0, n_blocks, bwd, 0)


def _gla(proj, lr, w_a2_pad, b_a, g_head, i_mix, *, seq, n_mix, rope=None, s0=None, prev_s=None):
    m = proj.shape[0]
    nb = m // seq
    latent = s0 is not None
    kq = GLA_KEY_DIM // GLA_DK
    kv = 2 * GLA_KEY_DIM // GLA_DV
    in_specs = [
        pl.BlockSpec((seq, GLA_DK), lambda b, h: (b, h)),
        pl.BlockSpec((seq, GLA_DK), lambda b, h: (b, kq + h)),
        pl.BlockSpec((seq, GLA_DV), lambda b, h: (b, kv + h)),
        pl.BlockSpec((seq, GLA_DV), lambda b, h: (b, kv + GLA_HEADS + h)),
        pl.BlockSpec((seq, LANES), lambda b, h: (b, 0)),
        pl.BlockSpec((None, 2, LANES, GLA_DK), lambda b, h: (i_mix, 0, 0, h)),
        pl.BlockSpec((None, 2, 1, GLA_DK), lambda b, h: (i_mix, 0, 0, h)),
        pl.BlockSpec((None, 1, GLA_DV), lambda b, h: (i_mix, 0, 0)),
    ]
    args = [proj, proj, proj, proj, lr, w_a2_pad, b_a, g_head]
    a_spec = pl.BlockSpec((seq, GLA_DV), lambda b, h: (b, h))
    a_shape = jax.ShapeDtypeStruct((m, GLA_HEADS * GLA_DV), BF16)
    scratch = [pltpu.VMEM((seq, GLA_DK), F32), pltpu.VMEM((seq, GLA_DK), F32),
               pltpu.VMEM((2, seq, GLA_DK), F32), pltpu.VMEM((seq, GLA_DV), F32)]
    aliases = {}
    if latent:
        in_specs += [
            pl.BlockSpec((seq, GLA_DK), lambda b, h: (0, 0)),
            pl.BlockSpec((seq, GLA_DK), lambda b, h: (0, 0)),
            pl.BlockSpec((1, 1, 2, 1, GLA_DK, GLA_DV), lambda b, h: (b, i_mix, 0, h, 0, 0)),
        ]
        args += [rope[0], rope[1], s0]
        out_specs, out_shape = a_spec, a_shape
        scratch.append(pltpu.VMEM((GLA_DV, GLA_DK), F32))
    else:
        out_specs = [a_spec, pl.BlockSpec((1, None, 2, 1, GLA_DK, GLA_DV),
                                          lambda b, h: (b, i_mix, 0, h, 0, 0))]
        out_shape = [a_shape,
                     jax.ShapeDtypeStruct((nb, n_mix, 2, GLA_HEADS, GLA_DK, GLA_DV), F32)]
        if prev_s is not None:
            in_specs.append(pl.BlockSpec(memory_space=pl.ANY))
            args.append(prev_s)
            aliases = {len(args) - 1: 1}
    body = functools.partial(_gla_body, seq=seq, latent=latent)
    if not latent and prev_s is not None:
        body = _drop_carried(body, n_in=len(args))
    return pl.pallas_call(
        body,
        grid=(nb, GLA_HEADS),
        in_specs=in_specs,
        out_specs=out_specs,
        out_shape=out_shape,
        scratch_shapes=scratch,
        input_output_aliases=aliases,
        compiler_params=_params(2),
        name="gla_sample" if latent else "gla_prompt",
    )(*args)


def _drop_carried(body, n_in):
    def wrapped(*refs):
        return body(*refs[:n_in - 1], *refs[n_in:])
    return wrapped


def kernel(x_prompt, x_sample, cache_na_k, cache_na_v, state_gla, c, c_ctx, w_ada, b_ada, norm_g,
           w_qkv_na, w_o_na, rpb_na, w_qkvg_gla, w_a1_gla, w_a2_gla, b_a_gla, g_head_gla, w_o_gla,
           w1_mlp, w2_mlp):
    n_p, seq_p, _ = x_prompt.shape
    n_s, seq_s, _ = x_sample.shape
    assert seq_s == TM and TM % seq_p == 0 and 1 + n_s <= MOD_ROWS
    xp = x_prompt.reshape(n_p * seq_p, D_MODEL)
    xs = x_sample.reshape(n_s * seq_s, D_MODEL)

    cond = jnp.zeros((MOD_ROWS, D_MODEL), F32).at[0].set(c_ctx).at[1:1 + n_s].set(c)
    mods = _ada_all(cond, w_ada, b_ada)
    norm_g4 = norm_g.reshape(DEPTH, 4, 1, D_MODEL)

    n_na = w_qkv_na.shape[0]
    rpb_pad = jnp.pad(rpb_na, ((0, 0), (0, 0), (0, N_DR - rpb_na.shape[2]),
                               (0, LANES - rpb_na.shape[3])))

    n_gla = w_a1_gla.shape[0]
    w_a1_pad = jnp.zeros((n_gla, D_MODEL, LANES), F32).at[:, :, :2 * GLA_RANK].set(
        w_a1_gla.transpose(0, 2, 1, 3).reshape(n_gla, D_MODEL, 2 * GLA_RANK))
    w_a2_pad = jnp.zeros((n_gla, 2, LANES, GLA_KEY_DIM), F32)
    for d in range(2):
        w_a2_pad = w_a2_pad.at[:, d, d * GLA_RANK:(d + 1) * GLA_RANK].set(w_a2_gla[:, d])
    b_a4 = b_a_gla.reshape(n_gla, 2, 1, GLA_KEY_DIM)
    g_head3 = g_head_gla.reshape(n_gla, 1, GLA_DV)
    rope = _rope_tables(seq_s)

    new_kv, new_s = (), None
    for l in range(DEPTH):
        i = l // N_MIXERS
        if l % N_MIXERS == 0:
            q_p, k_new, v_new = _qkv_prompt(xp, norm_g4, mods, w_qkv_na, l, i, seq_p, n_na, new_kv)
            new_kv = (k_new, v_new)
            a_p = _ctx_attn(q_p, k_new, v_new, i)
            qkv_s = _qkv_sample(xs, norm_g4, mods, w_qkv_na, l, i, seq_s)
            a_s = _na_attn(qkv_s, cache_na_k, cache_na_v, rpb_pad, i)
            w_o = w_o_na
        else:
            proj_p, lr_p = _gla_proj(xp, norm_g4, mods, w_qkvg_gla, w_a1_pad[i], l, i,
                                     row0=0, row_stride=0, name="gla_proj_prompt")
            a_p, new_s = _gla(proj_p, lr_p, w_a2_pad, b_a4, g_head3, i, seq=seq_p, n_mix=n_gla,
                              prev_s=new_s)
            proj_s, lr_s = _gla_proj(xs, norm_g4, mods, w_qkvg_gla, w_a1_pad[i], l, i,
                                     row0=1, row_stride=TM // seq_s, name="gla_proj_sample")
            a_s = _gla(proj_s, lr_s, w_a2_pad, b_a4, g_head3, i, seq=seq_s, n_mix=n_gla,
                       rope=rope, s0=state_gla)
            w_o = w_o_gla
        xp = _oproj(a_p, w_o, i, xp, norm_g4, mods, l, row0=0, n_groups=1, name="oproj_prompt")
        xs = _oproj(a_s, w_o, i, xs, norm_g4, mods, l, row0=1, n_groups=n_s, name="oproj_sample")
        xp = _mlp(xp, norm_g4, mods, w1_mlp, w2_mlp, l, row0=0, row_stride=0, name="mlp_prompt")
        xs = _mlp(xs, norm_g4, mods, w1_mlp, w2_mlp, l, row0=1, row_stride=TM // seq_s,
                  name="mlp_sample")

    return (xp.reshape(n_p, seq_p, D_MODEL), xs.reshape(n_s, seq_s, D_MODEL),
            new_kv[0], new_kv[1], new_s)
```
